```python
import math
import jax, jax.numpy as jnp
from jax import lax
import numpy as np


D_MODEL = 2048
BATCH = 4
SEQ = 4096
DEPTH = 4

N_MIXERS = 3
EPS = 1e-6

M_HEADS = 4
M_DQK = 256
M_DV = 512
M_CHUNK = 64
M_GATE_CAP = 15.0
M_COLS = M_HEADS * (2 * M_DQK + 2 * M_DV) + 2 * M_HEADS

A_HEADS = 32
A_KV_HEADS = 4
A_HEAD_DIM = 64
WINDOW = 128
A_COLS = (A_HEADS + 2 * A_KV_HEADS) * A_HEAD_DIM

R_HEADS = 8
R_DQK = 256
R_DV = 512
R_CHUNK = 128
R_COLS = R_HEADS * (2 * R_DQK + 2 * R_DV)

D_FF = 5632
CONV_WIDTH = 3

N_A = (DEPTH + 2) // 3
N_B = (DEPTH + 1) // 3
N_C = DEPTH // 3

kernel_name = 'hybrid_mlstm_swa_retention_convffn'


def rms_norm(x, gain):
    xf = x.astype(jnp.float32)
    y = xf * lax.rsqrt(jnp.mean(xf * xf, axis=-1, keepdims=True) + EPS)
    return (y * gain).astype(x.dtype)


def mlstm_mixer(x, w_in, gate_b, head_norm, w_out):
    f32 = jnp.float32
    bsz, seq, _ = x.shape
    H, dk, dv, L = M_HEADS, M_DQK, M_DV, M_CHUNK
    nc = seq // L
    proj = x @ w_in
    q, k, v, o, ig, fg = jnp.split(
        proj, [H * dk, 2 * H * dk, 2 * H * dk + H * dv, 2 * H * (dk + dv), 2 * H * (dk + dv) + H], axis=-1)
    ig = M_GATE_CAP * jnp.tanh((ig + gate_b[0]).astype(f32) / M_GATE_CAP)
    fg = M_GATE_CAP * jnp.tanh((fg + gate_b[1]).astype(f32) / M_GATE_CAP)
    logf = jax.nn.log_sigmoid(fg)

    def heads(t, d):
        return t.astype(f32).reshape(bsz, nc, L, H, d).transpose(1, 0, 3, 2, 4)

    def gates(t):
        return t.reshape(bsz, nc, L, H).transpose(1, 0, 3, 2)

    qc = heads(q, dk) * dk ** -0.5
    kc = heads(k, dk)
    vc = heads(v, dv)
    igc = gates(ig)
    lfc = gates(logf)
    causal = jnp.tril(jnp.ones((L, L), dtype=bool))

    def step(carry, inp):
        C, n, m = carry
        qb, kb, vb, ib, fb = inp
        b = jnp.cumsum(fb, axis=-1)
        dmat = jnp.where(causal, b[..., :, None] - b[..., None, :] + ib[..., None, :], -jnp.inf)
        inter = b + m[..., None]
        m_t = jnp.maximum(inter, jnp.max(dmat, axis=-1))
        w_intra = jnp.exp(dmat - m_t[..., None])
        w_inter = jnp.exp(inter - m_t)
        s = jnp.einsum('bhtd,bhsd->bhts', qb, kb) * w_intra
        num = jnp.einsum('bhts,bhsv->bhtv', s, vb) + w_inter[..., None] * jnp.einsum('bhtd,bhvd->bhtv', qb, C)
        den = jnp.sum(s, axis=-1) + w_inter * jnp.einsum('bhtd,bhd->bht', qb, n)
        h = num / jnp.maximum(jnp.abs(den), jnp.exp(-m_t))[..., None]
        g = b[..., -1]
        a = g[..., None] - b + ib
        m_new = jnp.maximum(g + m, jnp.max(a, axis=-1))
        keep = jnp.exp(g + m - m_new)
        wk = jnp.exp(a - m_new[..., None])
        C_new = keep[..., None, None] * C + jnp.einsum('bhs,bhsv,bhsd->bhvd', wk, vb, kb)
        n_new = keep[..., None] * n + jnp.einsum('bhs,bhsd->bhd', wk, kb)
        return (C_new, n_new, m_new), h

    init = (jnp.zeros((bsz, H, dv, dk), f32), jnp.zeros((bsz, H, dk), f32), jnp.zeros((bsz, H), f32))
    _, hc = lax.scan(step, init, (qc, kc, vc, igc, lfc))
    h = hc.transpose(1, 0, 3, 2, 4).reshape(bsz, seq, H, dv)
    h = h * lax.rsqrt(jnp.mean(h * h, axis=-1, keepdims=True) + EPS) * head_norm.reshape(H, dv).astype(f32)
    h = h.reshape(bsz, seq, H * dv) * jax.nn.sigmoid(o.astype(f32))
    return h.astype(x.dtype) @ w_out


def swa_sink_mixer(x, w_qkv, sinks, w_out):
    f32 = jnp.float32
    bsz, seq, _ = x.shape
    H, KV, hd, W = A_HEADS, A_KV_HEADS, A_HEAD_DIM, WINDOW
    G = H // KV
    nb = seq // W
    qkv = x @ w_qkv
    q, k, v = jnp.split(qkv, [H * hd, (H + KV) * hd], axis=-1)
    q = q.reshape(bsz, nb, W, KV, G, hd) * hd ** -0.5
    k = k.reshape(bsz, nb, W, KV, hd)
    v = v.reshape(bsz, nb, W, KV, hd)

    def with_prev(t):
        prev = jnp.pad(t[:, :-1], ((0, 0), (1, 0), (0, 0), (0, 0), (0, 0)))
        return jnp.concatenate([prev, t], axis=2)

    kb, vb = with_prev(k), with_prev(v)
    scores = jnp.einsum('bnqkgd,bnskd->bnkgqs', q, kb).astype(f32)
    qi = jnp.arange(W)[:, None]
    kj = jnp.arange(2 * W)[None, :]
    rel = qi + W - kj
    band = (rel >= 0) & (rel < W)
    key_abs = jnp.arange(nb)[:, None, None] * W + kj[None] - W
    valid = band[None] & (key_abs >= 0)
    scores = jnp.where(valid[None, :, None, None], scores, -jnp.inf)
    sink = sinks.astype(f32).reshape(KV, G)[None, None, :, :, None]
    mx = jnp.maximum(jnp.max(scores, axis=-1), sink)
    p = jnp.exp(scores - mx[..., None])
    denom = jnp.sum(p, axis=-1) + jnp.exp(sink - mx)
    p = p / denom[..., None]
    out = jnp.einsum('bnkgqs,bnskd->bnqkgd', p.astype(v.dtype), vb).reshape(bsz, seq, H * hd)
    return out @ w_out


def rotate(t, pos):
    half = t.shape[-1] // 2
    inv = jnp.power(10000.0, -jnp.arange(half, dtype=jnp.float32) / half)
    ang = pos[:, None] * inv[None, :]
    cos = jnp.cos(ang)[None, :, None, :]
    sin = jnp.sin(ang)[None, :, None, :]
    t1, t2 = t[..., :half], t[..., half:]
    return jnp.concatenate([t1 * cos - t2 * sin, t1 * sin + t2 * cos], axis=-1)


def retention_mixer(x, w_in, head_norm, w_out):
    f32 = jnp.float32
    bsz, seq, _ = x.shape
    H, dk, dv, L = R_HEADS, R_DQK, R_DV, R_CHUNK
    nc = seq // L
    proj = x @ w_in
    q, k, v, g = jnp.split(proj, [H * dk, 2 * H * dk, 2 * H * dk + H * dv], axis=-1)
    pos = jnp.arange(seq, dtype=f32)
    q = rotate(q.astype(f32).reshape(bsz, seq, H, dk), pos)
    k = rotate(k.astype(f32).reshape(bsz, seq, H, dk), pos) * dk ** -0.5
    v = v.astype(f32).reshape(bsz, seq, H, dv)

    def chunks(t, d):
        return t.reshape(bsz, nc, L, H, d).transpose(1, 0, 3, 2, 4)

    qc, kc, vc = chunks(q, dk), chunks(k, dk), chunks(v, dv)
    lg = jnp.log(1.0 - jnp.power(2.0, -5.0 - jnp.arange(H, dtype=f32)))
    lp = jnp.arange(L, dtype=f32)
    rel = lp[:, None] - lp[None, :]
    dmask = jnp.where(rel >= 0, jnp.exp(lg[:, None, None] * jnp.maximum(rel, 0.0)), 0.0)
    inter_decay = jnp.exp(lg[:, None] * (lp + 1.0))
    state_w = jnp.exp(lg[:, None] * (L - 1.0 - lp))
    chunk_decay = jnp.exp(lg * L)

    def step(R, inp):
        qb, kb, vb = inp
        s = jnp.einsum('bhtd,bhsd->bhts', qb, kb) * dmask
        out = jnp.einsum('bhts,bhsv->bhtv', s, vb) + jnp.einsum('bhtd,bhdv->bhtv', qb, R) * inter_decay[..., None]
        R_new = chunk_decay[:, None, None] * R + jnp.einsum('bhsd,hs,bhsv->bhdv', kb, state_w, vb)
        return R_new, out

    _, oc = lax.scan(step, jnp.zeros((bsz, H, dk, dv), f32), (qc, kc, vc))
    o = oc.transpose(1, 0, 3, 2, 4).reshape(bsz, seq, H, dv)
    mu = jnp.mean(o, axis=-1, keepdims=True)
    var = jnp.mean(jnp.square(o - mu), axis=-1, keepdims=True)
    o = (o - mu) * lax.rsqrt(var + EPS) * head_norm.reshape(H, dv).astype(f32)
    o = o.reshape(bsz, seq, H * dv) * jax.nn.silu(g.astype(f32))
    return o.astype(x.dtype) @ w_out


def conv_ffn(x, w_up, conv_w, conv_b, w_down):
    seq = x.shape[1]
    u = x @ w_up
    up = jnp.pad(u, ((0, 0), (CONV_WIDTH - 1, 0), (0, 0)))
    c = conv_b + sum(conv_w[j] * up[:, j:j + seq] for j in range(CONV_WIDTH))
    gate, val = jnp.split(c, 2, axis=-1)
    return (jax.nn.silu(gate) * val) @ w_down


def setup_inputs(seed: int = 0) -> dict:
    key = jax.random.key(seed)
    ks = jax.random.split(key, 20)
    f32 = jnp.float32

    def dense(k, shape, fan_in):
        return jax.random.normal(k, shape, f32) * fan_in ** -0.5

    def gain(k, shape):
        return 1.0 + 0.02 * jax.random.normal(k, shape, f32)

    x = jax.random.normal(ks[0], (BATCH, SEQ, D_MODEL), f32)
    norm_gains = gain(ks[1], (DEPTH, 4, D_MODEL))
    ffn_w_up = dense(ks[2], (DEPTH, D_MODEL, 2 * D_FF), D_MODEL)
    ffn_conv_w = dense(ks[3], (DEPTH, CONV_WIDTH, 2 * D_FF), CONV_WIDTH)
    ffn_conv_b = 0.02 * jax.random.normal(ks[4], (DEPTH, 2 * D_FF), f32)
    ffn_w_down = dense(ks[5], (DEPTH, D_FF, D_MODEL), D_FF)
    mlstm_w_in = dense(ks[6], (N_A, D_MODEL, M_COLS), D_MODEL)
    ig_b = 0.1 * jax.random.normal(ks[7], (N_A, M_HEADS), f32)
    fg_b = jnp.linspace(3.0, 6.0, M_HEADS, dtype=f32)[None] + 0.1 * jax.random.normal(ks[8], (N_A, M_HEADS), f32)
    mlstm_gate_b = jnp.stack([ig_b, fg_b], axis=1)
    mlstm_head_norm = gain(ks[9], (N_A, M_HEADS * M_DV))
    mlstm_w_out = dense(ks[10], (N_A, M_HEADS * M_DV, D_MODEL), M_HEADS * M_DV)
    swa_w_qkv = dense(ks[11], (N_B, D_MODEL, A_COLS), D_MODEL)
    swa_sinks = 0.5 * jax.random.normal(ks[12], (N_B, A_HEADS), f32)
    swa_w_out = dense(ks[13], (N_B, A_HEADS * A_HEAD_DIM, D_MODEL), A_HEADS * A_HEAD_DIM)
    ret_w_in = dense(ks[14], (N_C, D_MODEL, R_COLS), D_MODEL)
    ret_head_norm = gain(ks[15], (N_C, R_HEADS * R_DV))
    ret_w_out = dense(ks[16], (N_C, R_HEADS * R_DV, D_MODEL), R_HEADS * R_DV)
    return {'x': x, 'norm_gains': norm_gains, 'ffn_w_up': ffn_w_up, 'ffn_conv_w': ffn_conv_w,
            'ffn_conv_b': ffn_conv_b, 'ffn_w_down': ffn_w_down, 'mlstm_w_in': mlstm_w_in,
            'mlstm_gate_b': mlstm_gate_b, 'mlstm_head_norm': mlstm_head_norm, 'mlstm_w_out': mlstm_w_out,
            'swa_w_qkv': swa_w_qkv, 'swa_sinks': swa_sinks, 'swa_w_out': swa_w_out,
            'ret_w_in': ret_w_in, 'ret_head_norm': ret_head_norm, 'ret_w_out': ret_w_out}


def reference(x, norm_gains, ffn_w_up, ffn_conv_w, ffn_conv_b, ffn_w_down, mlstm_w_in, mlstm_gate_b,
              mlstm_head_norm, mlstm_w_out, swa_w_qkv, swa_sinks, swa_w_out, ret_w_in, ret_head_norm, ret_w_out):
    h = x
    for i in range(DEPTH):
        kind, j = i % N_MIXERS, i // N_MIXERS
        a = rms_norm(h, norm_gains[i, 0])
        if kind == 0:
            a = mlstm_mixer(a, mlstm_w_in[j], mlstm_gate_b[j], mlstm_head_norm[j], mlstm_w_out[j])
        elif kind == 1:
            a = swa_sink_mixer(a, swa_w_qkv[j], swa_sinks[j], swa_w_out[j])
        else:
            a = retention_mixer(a, ret_w_in[j], ret_head_norm[j], ret_w_out[j])
        h = h + rms_norm(a, norm_gains[i, 1])
        f = conv_ffn(rms_norm(h, norm_gains[i, 2]), ffn_w_up[i], ffn_conv_w[i], ffn_conv_b[i], ffn_w_down[i])
        h = h + rms_norm(f, norm_gains[i, 3])
    return h
```

```python
import functools
import math

import jax
import jax.numpy as jnp
from jax import lax
from jax.experimental import pallas as pl
from jax.experimental.pallas import tpu as pltpu

F32 = jnp.float32
BF16 = jnp.bfloat16
EPS = 1e-6
LANES = 128
VMEM_LIMIT_BYTES = 56 * 1024 * 1024

M_HEADS, M_DQK, M_DV, M_GATE_CAP = 4, 256, 512, 15.0
A_HEADS, A_KV_HEADS, A_HEAD_DIM, WINDOW = 32, 4, 64, 128
R_HEADS, R_DQK, R_DV = 8, 256, 512
CONV_WIDTH = 3
CHUNK = 256


def _params(*semantics):
    return pltpu.CompilerParams(dimension_semantics=semantics, vmem_limit_bytes=VMEM_LIMIT_BYTES)


def _rms(y, gain):
    return y * lax.rsqrt(jnp.mean(y * y, axis=-1, keepdims=True) + EPS) * gain


def _rmsnorm_kernel(x_ref, g_ref, o_ref):
    o_ref[...] = _rms(x_ref[...], g_ref[...]).astype(o_ref.dtype)


def rmsnorm(x, gain, tm=512):
    t, d = x.shape
    return pl.pallas_call(
        _rmsnorm_kernel,
        grid=(t // tm,),
        in_specs=[pl.BlockSpec((tm, d), lambda i: (i, 0)), pl.BlockSpec((1, d), lambda i: (0, 0))],
        out_specs=pl.BlockSpec((tm, d), lambda i: (i, 0)),
        out_shape=jax.ShapeDtypeStruct((t, d), BF16),
        compiler_params=_params("parallel"),
        name="rmsnorm",
    )(x, gain.reshape(1, d))


def _matmul_kernel(x_ref, w_ref, o_ref):
    o_ref[...] = jnp.dot(x_ref[...], w_ref[...], preferred_element_type=F32).astype(o_ref.dtype)


def matmul(x, w, out_dtype, tm, tn, name):
    t, k = x.shape
    n = w.shape[1]
    return pl.pallas_call(
        _matmul_kernel,
        grid=(n // tn, t // tm),
        in_specs=[pl.BlockSpec((tm, k), lambda j, i: (i, 0)), pl.BlockSpec((k, tn), lambda j, i: (0, j))],
        out_specs=pl.BlockSpec((tm, tn), lambda j, i: (i, j)),
        out_shape=jax.ShapeDtypeStruct((t, n), out_dtype),
        compiler_params=_params("parallel", "parallel"),
        name=name,
    )(x, w)


def _out_proj_kernel(a_ref, w_ref, h_ref, g_ref, gn_ref, hout_ref, *rest, emit_next):
    acc_ref = rest[-1]
    kk = pl.program_id(1)

    @pl.when(kk == 0)
    def _():
        acc_ref[...] = jnp.zeros_like(acc_ref)

    acc_ref[...] += jnp.dot(a_ref[...], w_ref[...], preferred_element_type=F32)

    @pl.when(kk == pl.num_programs(1) - 1)
    def _():
        hn = h_ref[...] + _rms(acc_ref[...], g_ref[...])
        hout_ref[...] = hn
        if emit_next:
            rest[0][...] = _rms(hn, gn_ref[...]).astype(BF16)


def out_proj(a, w, h, gain, next_gain, tm, tk, name):
    t, k = a.shape
    d = w.shape[1]
    emit_next = next_gain is not None
    gn = next_gain if emit_next else gain
    row = pl.BlockSpec((tm, d), lambda i, kk: (i, 0))
    vec = pl.BlockSpec((1, d), lambda i, kk: (0, 0))
    out_shape = [jax.ShapeDtypeStruct((t, d), F32)]
    out_specs = [row]
    if emit_next:
        out_shape.append(jax.ShapeDtypeStruct((t, d), BF16))
        out_specs.append(row)
    res = pl.pallas_call(
        functools.partial(_out_proj_kernel, emit_next=emit_next),
        grid=(t // tm, k // tk),
        in_specs=[pl.BlockSpec((tm, tk), lambda i, kk: (i, kk)), pl.BlockSpec((tk, d), lambda i, kk: (kk, 0)),
                  row, vec, vec],
        out_specs=out_specs,
        out_shape=out_shape,
        scratch_shapes=[pltpu.VMEM((tm, d), F32)],
        compiler_params=_params("parallel", "arbitrary"),
        name=name,
    )(a, w, h, gain.reshape(1, d), gn.reshape(1, d))
    return (res[0], res[1]) if emit_next else (res[0], None)


def _causal_conv(u, carry_ref, cw, cb, first):
    tm = u.shape[0]
    w0, w1, w2 = cw[0:1, :], cw[1:2, :], cw[2:3, :]
    main = cb + w0 * pltpu.roll(u, 2, axis=0) + w1 * pltpu.roll(u, 1, axis=0) + w2 * u
    prev = jnp.where(first, 0.0, carry_ref[...])
    head = u[0:8, :]
    rid = lax.broadcasted_iota(jnp.int32, head.shape, 0)
    p1 = jnp.where(rid == 0, pltpu.roll(prev, 1, axis=0), pltpu.roll(head, 1, axis=0))
    p2 = jnp.where(rid <= 1, pltpu.roll(prev, 2, axis=0), pltpu.roll(head, 2, axis=0))
    fix = cb + w0 * p2 + w1 * p1 + w2 * head
    carry_ref[...] = u[tm - 8:tm, :]
    return main, fix


def _ffn_up_kernel(x_ref, wg_ref, wv_ref, cwg_ref, cwv_ref, cbg_ref, cbv_ref, o_ref, cg_ref, cv_ref, *, tiles_per_seq):
    first = pl.program_id(1) % tiles_per_seq == 0
    x = x_ref[...]
    ug = jnp.dot(x, wg_ref[...], preferred_element_type=F32)
    uv = jnp.dot(x, wv_ref[...], preferred_element_type=F32)
    g_main, g_fix = _causal_conv(ug, cg_ref, cwg_ref[...], cbg_ref[...], first)
    v_main, v_fix = _causal_conv(uv, cv_ref, cwv_ref[...], cbv_ref[...], first)
    o_ref[...] = (g_main * jax.nn.sigmoid(g_main) * v_main).astype(o_ref.dtype)
    o_ref[0:8, :] = (g_fix * jax.nn.sigmoid(g_fix) * v_fix).astype(o_ref.dtype)


def ffn_up(xn, w_up, conv_w, conv_b, seq, tm, tn):
    t, d = xn.shape
    f = w_up.shape[1] // 2
    nj = f // tn
    cb = conv_b.reshape(1, 2 * f)
    return pl.pallas_call(
        functools.partial(_ffn_up_kernel, tiles_per_seq=seq // tm),
        grid=(nj, t // tm),
        in_specs=[
            pl.BlockSpec((tm, d), lambda j, i: (i, 0)),
            pl.BlockSpec((d, tn), lambda j, i: (0, j)),
            pl.BlockSpec((d, tn), lambda j, i: (0, j + nj)),
            pl.BlockSpec((CONV_WIDTH, tn), lambda j, i: (0, j)),
            pl.BlockSpec((CONV_WIDTH, tn), lambda j, i: (0, j + nj)),
            pl.BlockSpec((1, tn), lambda j, i: (0, j)),
            pl.BlockSpec((1, tn), lambda j, i: (0, j + nj)),
        ],
        out_specs=pl.BlockSpec((tm, tn), lambda j, i: (i, j)),
        out_shape=jax.ShapeDtypeStruct((t, f), BF16),
        scratch_shapes=[pltpu.VMEM((8, tn), F32), pltpu.VMEM((8, tn), F32)],
        compiler_params=_params("parallel", "arbitrary"),
        name="ffn_up",
    )(xn, w_up, w_up, conv_w, conv_w, cb, cb)


def _log_sigmoid(x):
    return jnp.minimum(x, 0.0) - jnp.log1p(jnp.exp(-jnp.abs(x)))


def _dot_nt(a, b):
    return lax.dot_general(a, b, (((1,), (1,)), ((), ())), preferred_element_type=F32)


def _mlstm_kernel(q_ref, k_ref, v_ref, o_ref, gates_ref, gb_ref, hn_ref, out_ref, ct_ref, m_ref):
    L = q_ref.shape[0]
    dk, dv = M_DQK, M_DV

    @pl.when(pl.program_id(1) == 0)
    def _():
        ct_ref[...] = jnp.zeros_like(ct_ref)
        m_ref[...] = jnp.zeros_like(m_ref)

    act = M_GATE_CAP * jnp.tanh((gates_ref[...] + gb_ref[...]) / M_GATE_CAP)
    lane = lax.broadcasted_iota(jnp.int32, act.shape, 1)
    gval = jnp.where(lane >= M_HEADS, _log_sigmoid(act), act)
    rows = lax.broadcasted_iota(jnp.int32, (L, L), 0)
    cols = lax.broadcasted_iota(jnp.int32, (L, L), 1)
    causal = rows >= cols
    cum = jnp.dot(causal.astype(F32), gval, preferred_element_type=F32, precision=lax.Precision.HIGHEST)
    gval_t = gval.T
    cum_t = cum.T
    ones_col = (lax.broadcasted_iota(jnp.int32, (L, LANES), 1) == 0).astype(BF16)
    scale = dk ** -0.5

    for h in range(M_HEADS):
        b_col = cum[:, M_HEADS + h:M_HEADS + h + 1]
        b_row = cum_t[M_HEADS + h:M_HEADS + h + 1, :]
        i_row = gval_t[h:h + 1, :]
        m_prev = m_ref[h:h + 1, 0:1]
        q = q_ref[:, h * dk:(h + 1) * dk]
        k = k_ref[:, h * dk:(h + 1) * dk]
        v_ext = jnp.concatenate([v_ref[:, h * dv:(h + 1) * dv], ones_col], axis=1)
        ct = ct_ref[h]

        dmat = jnp.where(causal, b_col - b_row + i_row, -jnp.inf)
        inter = b_col + m_prev
        m_t = jnp.maximum(inter, jnp.max(dmat, axis=1, keepdims=True))
        w_intra = jnp.exp(dmat - m_t) * scale
        w_inter = jnp.exp(inter - m_t) * scale
        s = (_dot_nt(q, k) * w_intra).astype(BF16)
        num_ext = (jnp.dot(s, v_ext, preferred_element_type=F32)
                   + w_inter * jnp.dot(q, ct.astype(BF16), preferred_element_type=F32))
        den = num_ext[:, dv:dv + 1]
        hh = num_ext[:, :dv] * (1.0 / jnp.maximum(jnp.abs(den), jnp.exp(-m_t)))
        hh = _rms(hh, hn_ref[:, h * dv:(h + 1) * dv])
        hh = hh * jax.nn.sigmoid(o_ref[:, h * dv:(h + 1) * dv].astype(F32))
        out_ref[:, h * dv:(h + 1) * dv] = hh.astype(out_ref.dtype)

        g = b_col[L - 1:L, :]
        a_row = g - b_row + i_row
        m_new = jnp.maximum(g + m_prev, jnp.max(a_row, axis=1, keepdims=True))
        keep = jnp.exp(g + m_prev - m_new)
        wk_row = jnp.exp(a_row - m_new)
        ktw = (k.astype(F32).T * wk_row).astype(BF16)
        ct_ref[h] = keep * ct + jnp.dot(ktw, v_ext, preferred_element_type=F32)
        m_ref[h:h + 1, :] = jnp.broadcast_to(m_new, (1, LANES))


def mlstm_core(proj, gates, gate_b, head_norm, bsz, seq):
    t = proj.shape[0]
    hq, hv = M_HEADS * M_DQK, M_HEADS * M_DV
    nc = seq // CHUNK
    gb = jnp.zeros((1, LANES), F32).at[0, :2 * M_HEADS].set(gate_b.reshape(-1))
    row = lambda b, c: b * nc + c
    return pl.pallas_call(
        _mlstm_kernel,
        grid=(bsz, nc),
        in_specs=[
            pl.BlockSpec((CHUNK, hq), lambda b, c: (row(b, c), 0)),
            pl.BlockSpec((CHUNK, hq), lambda b, c: (row(b, c), 1)),
            pl.BlockSpec((CHUNK, hv), lambda b, c: (row(b, c), 2 * hq // hv)),
            pl.BlockSpec((CHUNK, hv), lambda b, c: (row(b, c), 2 * hq // hv + 1)),
            pl.BlockSpec((CHUNK, LANES), lambda b, c: (row(b, c), 0)),
            pl.BlockSpec((1, LANES), lambda b, c: (0, 0)),
            pl.BlockSpec((1, hv), lambda b, c: (0, 0)),
        ],
        out_specs=pl.BlockSpec((CHUNK, hv), lambda b, c: (row(b, c), 0)),
        out_shape=jax.ShapeDtypeStruct((t, hv), BF16),
        scratch_shapes=[pltpu.VMEM((M_HEADS, M_DQK, M_DV + LANES), F32), pltpu.VMEM((8, LANES), F32)],
        compiler_params=_params("parallel", "arbitrary"),
        name="mlstm_core",
    )(proj, proj, proj, proj, gates, gb, head_norm.reshape(1, hv))


def _rotate(t, cos, sin):
    half = t.shape[1] // 2
    t1, t2 = t[:, :half], t[:, half:]
    return jnp.concatenate([t1 * cos - t2 * sin, t1 * sin + t2 * cos], axis=1)


def _retention_kernel(q_ref, k_ref, v_ref, g_ref, cos_ref, sin_ref, hn_ref, out_ref, r_ref):
    L = q_ref.shape[0]
    dk, dv = R_DQK, R_DV

    @pl.when(pl.program_id(1) == 0)
    def _():
        r_ref[...] = jnp.zeros_like(r_ref)

    cos, sin = cos_ref[...], sin_ref[...]
    rel = (lax.broadcasted_iota(jnp.int32, (L, L), 0) - lax.broadcasted_iota(jnp.int32, (L, L), 1)).astype(F32)
    t_col = lax.broadcasted_iota(jnp.int32, (L, 1), 0).astype(F32)
    s_row = lax.broadcasted_iota(jnp.int32, (1, L), 1).astype(F32)
    scale = dk ** -0.5

    for h in range(R_HEADS):
        lg = math.log(1.0 - 2.0 ** (-5.0 - h))
        dmask = jnp.where(rel >= 0, jnp.exp(lg * jnp.maximum(rel, 0.0)), 0.0) * scale
        inter_decay = jnp.exp(lg * (t_col + 1.0))
        state_w = jnp.exp(lg * (L - 1.0 - s_row)) * scale
        qr = _rotate(q_ref[:, h * dk:(h + 1) * dk].astype(F32), cos, sin).astype(BF16)
        kr = _rotate(k_ref[:, h * dk:(h + 1) * dk].astype(F32), cos, sin)
        v = v_ref[:, h * dv:(h + 1) * dv]
        r = r_ref[h]
        s = (_dot_nt(qr, kr.astype(BF16)) * dmask).astype(BF16)
        out = (jnp.dot(s, v, preferred_element_type=F32)
               + jnp.dot(qr, r.astype(BF16), preferred_element_type=F32) * inter_decay)
        ktw = (kr.T * state_w).astype(BF16)
        r_ref[h] = math.exp(lg * L) * r + jnp.dot(ktw, v, preferred_element_type=F32)

        mu = jnp.mean(out, axis=-1, keepdims=True)
        cen = out - mu
        var = jnp.mean(cen * cen, axis=-1, keepdims=True)
        o = cen * lax.rsqrt(var + EPS) * hn_ref[:, h * dv:(h + 1) * dv]
        gate = g_ref[:, h * dv:(h + 1) * dv].astype(F32)
        out_ref[:, h * dv:(h + 1) * dv] = (o * (gate * jax.nn.sigmoid(gate))).astype(out_ref.dtype)


def retention_core(proj, head_norm, bsz, seq):
    t = proj.shape[0]
    hq, hv = R_HEADS * R_DQK, R_HEADS * R_DV
    nc = seq // CHUNK
    half = R_DQK // 2
    inv = jnp.power(10000.0, -jnp.arange(half, dtype=F32) / half)
    ang = jnp.arange(seq, dtype=F32)[:, None] * inv[None, :]
    row = lambda b, c: b * nc + c
    return pl.pallas_call(
        _retention_kernel,
        grid=(bsz, nc),
        in_specs=[
            pl.BlockSpec((CHUNK, hq), lambda b, c: (row(b, c), 0)),
            pl.BlockSpec((CHUNK, hq), lambda b, c: (row(b, c), 1)),
            pl.BlockSpec((CHUNK, hv), lambda b, c: (row(b, c), 2 * hq // hv)),
            pl.BlockSpec((CHUNK, hv), lambda b, c: (row(b, c), 2 * hq // hv + 1)),
            pl.BlockSpec((CHUNK, half), lambda b, c: (c, 0)),
            pl.BlockSpec((CHUNK, half), lambda b, c: (c, 0)),
            pl.BlockSpec((1, hv), lambda b, c: (0, 0)),
        ],
        out_specs=pl.BlockSpec((CHUNK, hv), lambda b, c: (row(b, c), 0)),
        out_shape=jax.ShapeDtypeStruct((t, hv), BF16),
        scratch_shapes=[pltpu.VMEM((R_HEADS, R_DQK, R_DV), F32)],
        compiler_params=_params("parallel", "arbitrary"),
        name="retention_core",
    )(proj, proj, proj, proj, jnp.cos(ang), jnp.sin(ang), head_norm.reshape(1, hv))


def _swa_kernel(sink_ref, q_ref, kp_ref, kc_ref, vp_ref, vc_ref, out_ref):
    W, hd = WINDOW, A_HEAD_DIM
    group = A_HEADS // A_KV_HEADS
    pairs = group // 2
    has_prev = pl.program_id(1) > 0
    lane = lax.broadcasted_iota(jnp.int32, (2 * W, LANES), 1)
    qi = lax.broadcasted_iota(jnp.int32, (pairs * W, 2 * W), 0) % W
    kj = lax.broadcasted_iota(jnp.int32, (pairs * W, 2 * W), 1)
    valid = (kj > qi) & (kj <= qi + W) & (has_prev | (kj >= W))

    def pair_layout(t, kv):
        slab = t[:, (kv // 2) * LANES:(kv // 2 + 1) * LANES].astype(F32)
        other = pltpu.roll(slab, hd, axis=1)
        own_low = (kv % 2 == 0)
        low = jnp.where(lane < hd, slab if own_low else other, 0.0)
        high = jnp.where(lane >= hd, other if own_low else slab, 0.0)
        return jnp.concatenate([low, high], axis=0).astype(BF16)

    kcat = jnp.concatenate([kp_ref[...], kc_ref[...]], axis=0)
    vcat = jnp.concatenate([vp_ref[...], vc_ref[...]], axis=0)
    for kv in range(A_KV_HEADS):
        kk = pair_layout(kcat, kv)
        vv = pair_layout(vcat, kv)
        base = kv * group * hd
        qs = jnp.concatenate([q_ref[:, base + p * LANES: base + (p + 1) * LANES] for p in range(pairs)], axis=0)
        sc = _dot_nt(qs, kk) * hd ** -0.5
        probs, inv = [], []
        for half in range(2):
            s_h = jnp.where(valid, sc[:, half * 2 * W:(half + 1) * 2 * W], -jnp.inf)
            rid = lax.broadcasted_iota(jnp.int32, (pairs * W, 1), 0) // W
            sink = jnp.zeros((pairs * W, 1), F32)
            for p in range(pairs):
                sink = jnp.where(rid == p, sink_ref[kv * group + 2 * p + half], sink)
            mx = jnp.maximum(jnp.max(s_h, axis=1, keepdims=True), sink)
            pr = jnp.exp(s_h - mx)
            den = jnp.sum(pr, axis=1, keepdims=True) + jnp.exp(sink - mx)
            probs.append(pr.astype(BF16))
            inv.append(1.0 / den)
        pv = jnp.dot(jnp.concatenate(probs, axis=1), vv, preferred_element_type=F32)
        lane_o = lax.broadcasted_iota(jnp.int32, pv.shape, 1)
        pv = pv * jnp.where(lane_o < hd, inv[0], inv[1])
        for p in range(pairs):
            out_ref[:, base + p * LANES: base + (p + 1) * LANES] = pv[p * W:(p + 1) * W, :].astype(out_ref.dtype)


def swa_core(qkv, sinks, bsz, seq):
    t = qkv.shape[0]
    hq, hkv = A_HEADS * A_HEAD_DIM, A_KV_HEADS * A_HEAD_DIM
    nb = seq // WINDOW
    cur = lambda b, n, s: b * nb + n
    prev = lambda b, n, s: b * nb + jnp.maximum(n - 1, 0)
    return pl.pallas_call(
        _swa_kernel,
        grid_spec=pltpu.PrefetchScalarGridSpec(
            num_scalar_prefetch=1,
            grid=(bsz, nb),
            in_specs=[
                pl.BlockSpec((WINDOW, hq), lambda b, n, s: (cur(b, n, s), 0)),
                pl.BlockSpec((WINDOW, hkv), lambda b, n, s: (prev(b, n, s), hq // hkv)),
                pl.BlockSpec((WINDOW, hkv), lambda b, n, s: (cur(b, n, s), hq // hkv)),
                pl.BlockSpec((WINDOW, hkv), lambda b, n, s: (prev(b, n, s), hq // hkv + 1)),
                pl.BlockSpec((WINDOW, hkv), lambda b, n, s: (cur(b, n, s), hq // hkv + 1)),
            ],
            out_specs=pl.BlockSpec((WINDOW, hq), lambda b, n, s: (cur(b, n, s), 0)),
        ),
        out_shape=jax.ShapeDtypeStruct((t, hq), BF16),
        compiler_params=_params("parallel", "arbitrary"),
        name="swa_core",
    )(sinks, qkv, qkv, qkv, qkv, qkv)


def kernel(x, norm_gains, ffn_w_up, ffn_conv_w, ffn_conv_b, ffn_w_down, mlstm_w_in, mlstm_gate_b, mlstm_head_norm,
           mlstm_w_out, swa_w_qkv, swa_sinks, swa_w_out, ret_w_in, ret_head_norm, ret_w_out):
    bsz, seq, d = x.shape
    depth = norm_gains.shape[0]
    t = bsz * seq
    h = x.reshape(t, d)
    xn = rmsnorm(h, norm_gains[0, 0])
    for i in range(depth):
        kind, j = i % 3, i // 3
        if kind == 0:
            w_in = mlstm_w_in[j]
            n_main = 2 * M_HEADS * (M_DQK + M_DV)
            w_gates = jnp.pad(w_in[:, n_main:], ((0, 0), (0, LANES - 2 * M_HEADS)))
            proj = matmul(xn, w_in[:, :n_main].astype(BF16), BF16, 1024, 1024, "mlstm_in")
            gates = matmul(xn, w_gates.astype(BF16), F32, 1024, LANES, "mlstm_gates")
            a = mlstm_core(proj, gates, mlstm_gate_b[j], mlstm_head_norm[j], bsz, seq)
            w_out = mlstm_w_out[j]
        elif kind == 1:
            qkv = matmul(xn, swa_w_qkv[j].astype(BF16), BF16, 1024, 512, "swa_in")
            a = swa_core(qkv, swa_sinks[j], bsz, seq)
            w_out = swa_w_out[j]
        else:
            proj = matmul(xn, ret_w_in[j].astype(BF16), BF16, 1024, 1024, "ret_in")
            a = retention_core(proj, ret_head_norm[j], bsz, seq)
            w_out = ret_w_out[j]
        h, xn = out_proj(a, w_out.astype(BF16), h, norm_gains[i, 1], norm_gains[i, 2], 512, 1024, "mixer_out")
        act = ffn_up(xn, ffn_w_up[i].astype(BF16), ffn_conv_w[i], ffn_conv_b[i], seq, 512, 512)
        next_gain = norm_gains[i + 1, 0] if i + 1 < depth else None
        h, xn = out_proj(act, ffn_w_down[i].astype(BF16), h, norm_gains[i, 3], next_gain, 512, 1408, "ffn_down")
    return h.reshape(bsz, seq, d)
```

```python
import functools
import math

import jax
import jax.numpy as jnp
from jax import lax
from jax.experimental import pallas as pl
from jax.experimental.pallas import tpu as pltpu

F32 = jnp.float32
BF16 = jnp.bfloat16
EPS = 1e-6
LANES = 128
VMEM_LIMIT_BYTES = 56 * 1024 * 1024

M_HEADS, M_DQK, M_DV, M_GATE_CAP = 4, 256, 512, 15.0
A_HEADS, A_KV_HEADS, A_HEAD_DIM, WINDOW = 32, 4, 64, 128
R_HEADS, R_DQK, R_DV = 8, 256, 512
CONV_WIDTH = 3
CHUNK = 256


def _params(*semantics):
    return pltpu.CompilerParams(dimension_semantics=semantics, vmem_limit_bytes=VMEM_LIMIT_BYTES)


def _rms(y, gain):
    return y * lax.rsqrt(jnp.mean(y * y, axis=-1, keepdims=True) + EPS) * gain


def _rmsnorm_kernel(x_ref, g_ref, o_ref):
    o_ref[...] = _rms(x_ref[...], g_ref[...]).astype(o_ref.dtype)


def rmsnorm(x, gain, tm=512):
    t, d = x.shape
    return pl.pallas_call(
        _rmsnorm_kernel,
        grid=(t // tm,),
        in_specs=[pl.BlockSpec((tm, d), lambda i: (i, 0)), pl.BlockSpec((1, d), lambda i: (0, 0))],
        out_specs=pl.BlockSpec((tm, d), lambda i: (i, 0)),
        out_shape=jax.ShapeDtypeStruct((t, d), BF16),
        compiler_params=_params("parallel"),
        name="rmsnorm",
    )(x, gain.reshape(1, d))


def _matmul_kernel(x_ref, w_ref, o_ref, wb_ref):
    @pl.when(pl.program_id(1) == 0)
    def _():
        wb_ref[...] = w_ref[...].astype(BF16)

    o_ref[...] = jnp.dot(x_ref[...], wb_ref[...], preferred_element_type=F32).astype(o_ref.dtype)


def matmul(x, w_stack, layer, n, out_dtype, tm, tn, name):
    t, k = x.shape
    return pl.pallas_call(
        _matmul_kernel,
        grid=(n // tn, t // tm),
        in_specs=[pl.BlockSpec((tm, k), lambda j, i: (i, 0)), pl.BlockSpec((None, k, tn), lambda j, i: (layer, 0, j))],
        out_specs=pl.BlockSpec((tm, tn), lambda j, i: (i, j)),
        out_shape=jax.ShapeDtypeStruct((t, n), out_dtype),
        scratch_shapes=[pltpu.VMEM((k, tn), BF16)],
        compiler_params=_params("parallel", "arbitrary"),
        name=name,
    )(x, w_stack)


def _out_proj_step(a_ref, w_ref, h_ref, g_ref, gn_ref, hout_ref, xn_ref, y_prev_ref, y_cur_ref):
    y_cur_ref[...] = jnp.dot(a_ref[...], w_ref[...], preferred_element_type=F32)
    hn = h_ref[...] + _rms(y_prev_ref[...], g_ref[...])
    hout_ref[...] = hn
    if xn_ref is not None:
        xn_ref[...] = _rms(hn, gn_ref[...]).astype(BF16)


def _out_proj_kernel(a_ref, w_ref, h_ref, g_ref, gn_ref, hout_ref, *rest, emit_next):
    xn_ref = rest[0] if emit_next else None
    ya_ref, yb_ref = rest[-2:]
    i = pl.program_id(0)

    @pl.when(i == 0)
    def _():
        yb_ref[...] = jnp.zeros_like(yb_ref)

    @pl.when(i % 2 == 0)
    def _():
        _out_proj_step(a_ref, w_ref, h_ref, g_ref, gn_ref, hout_ref, xn_ref, yb_ref, ya_ref)

    @pl.when(i % 2 == 1)
    def _():
        _out_proj_step(a_ref, w_ref, h_ref, g_ref, gn_ref, hout_ref, xn_ref, ya_ref, yb_ref)


def out_proj(a, w, h, gain, next_gain, tm, name):
    t, k = a.shape
    d = w.shape[1]
    nt = t // tm
    emit_next = next_gain is not None
    gn = next_gain if emit_next else gain
    row = pl.BlockSpec((tm, d), lambda i: (jnp.maximum(i - 1, 0), 0))
    vec = pl.BlockSpec((1, d), lambda i: (0, 0))
    out_shape = [jax.ShapeDtypeStruct((t, d), F32)]
    out_specs = [row]
    if emit_next:
        out_shape.append(jax.ShapeDtypeStruct((t, d), BF16))
        out_specs.append(row)
    res = pl.pallas_call(
        functools.partial(_out_proj_kernel, emit_next=emit_next),
        grid=(nt + 1,),
        in_specs=[pl.BlockSpec((tm, k), lambda i: (jnp.minimum(i, nt - 1), 0)),
                  pl.BlockSpec((k, d), lambda i: (0, 0), pipeline_mode=pl.Buffered(1)),
                  row, vec, vec],
        out_specs=out_specs,
        out_shape=out_shape,
        scratch_shapes=[pltpu.VMEM((tm, d), F32), pltpu.VMEM((tm, d), F32)],
        compiler_params=_params("arbitrary"),
        name=name,
    )(a, w, h, gain.reshape(1, d), gn.reshape(1, d))
    return (res[0], res[1]) if emit_next else (res[0], None)


def _silu_mul(g, v):
    return g * jax.nn.sigmoid(g) * v


def _ffn_up_kernel(x_ref, wg_ref, wv_ref, cwg_ref, cwv_ref, cbg_ref, cbv_ref, o_ref, cg_ref, cv_ref, ug_ref, uv_ref, *,
                   tiles_per_seq, sub, rows):
    first = pl.program_id(1) % tiles_per_seq == 0
    tm, tn = o_ref.shape
    for c in range(tn // sub):
        sl = slice(c * sub, (c + 1) * sub)
        b = c % 2
        for u_ref, w_ref, carry_ref in ((ug_ref, wg_ref, cg_ref), (uv_ref, wv_ref, cv_ref)):
            u_ref[b, 0:8, :] = jnp.where(first, 0.0, carry_ref[:, sl])
            u_ref[b, 8:8 + tm, :] = jnp.dot(x_ref[...], w_ref[:, sl], preferred_element_type=F32)
            carry_ref[:, sl] = u_ref[b, tm:tm + 8, :]
        for r in range(0, tm, rows):
            def conv(u_ref, cw_ref, cb_ref):
                return (cb_ref[:, sl] + cw_ref[0:1, sl] * u_ref[b, pl.ds(6 + r, rows), :]
                        + cw_ref[1:2, sl] * u_ref[b, pl.ds(7 + r, rows), :]
                        + cw_ref[2:3, sl] * u_ref[b, pl.ds(8 + r, rows), :])
            act = _silu_mul(conv(ug_ref, cwg_ref, cbg_ref), conv(uv_ref, cwv_ref, cbv_ref))
            o_ref[r:r + rows, sl] = act.astype(o_ref.dtype)


def ffn_up(xn, w_up, conv_w, conv_b, seq, tm, tn, sub, rows=64):
    t, d = xn.shape
    f = w_up.shape[1] // 2
    nj = f // tn
    cb = conv_b.reshape(1, 2 * f)
    return pl.pallas_call(
        functools.partial(_ffn_up_kernel, tiles_per_seq=seq // tm, sub=sub, rows=rows),
        grid=(nj, t // tm),
        in_specs=[
            pl.BlockSpec((tm, d), lambda j, i: (i, 0)),
            pl.BlockSpec((d, tn), lambda j, i: (0, j), pipeline_mode=pl.Buffered(1)),
            pl.BlockSpec((d, tn), lambda j, i: (0, j + nj), pipeline_mode=pl.Buffered(1)),
            pl.BlockSpec((CONV_WIDTH, tn), lambda j, i: (0, j)),
            pl.BlockSpec((CONV_WIDTH, tn), lambda j, i: (0, j + nj)),
            pl.BlockSpec((1, tn), lambda j, i: (0, j)),
            pl.BlockSpec((1, tn), lambda j, i: (0, j + nj)),
        ],
        out_specs=pl.BlockSpec((tm, tn), lambda j, i: (i, j)),
        out_shape=jax.ShapeDtypeStruct((t, f), BF16),
        scratch_shapes=[pltpu.VMEM((8, tn), F32), pltpu.VMEM((8, tn), F32),
                        pltpu.VMEM((2, 8 + tm, sub), F32), pltpu.VMEM((2, 8 + tm, sub), F32)],
        compiler_params=_params("parallel", "arbitrary"),
        name="ffn_up",
    )(xn, w_up, w_up, conv_w, conv_w, cb, cb)


def _log_sigmoid(x):
    return jnp.minimum(x, 0.0) - jnp.log1p(jnp.exp(-jnp.abs(x)))


def _dot_nt(a, b):
    return lax.dot_general(a, b, (((1,), (1,)), ((), ())), preferred_element_type=F32)


def _mlstm_kernel(q_ref, k_ref, v_ref, o_ref, gates_ref, gb_ref, hn_ref, out_ref, ct_ref, m_ref):
    L = q_ref.shape[0]
    dk, dv = M_DQK, M_DV

    @pl.when(pl.program_id(1) == 0)
    def _():
        ct_ref[...] = jnp.zeros_like(ct_ref)
        m_ref[...] = jnp.zeros_like(m_ref)

    act = M_GATE_CAP * jnp.tanh((gates_ref[...] + gb_ref[...]) / M_GATE_CAP)
    lane = lax.broadcasted_iota(jnp.int32, act.shape, 1)
    gval = jnp.where(lane >= M_HEADS, _log_sigmoid(act), act)
    rows = lax.broadcasted_iota(jnp.int32, (L, L), 0)
    cols = lax.broadcasted_iota(jnp.int32, (L, L), 1)
    causal = rows >= cols
    cum = jnp.dot(causal.astype(F32), gval, preferred_element_type=F32, precision=lax.Precision.HIGHEST)
    gval_t = gval.T
    cum_t = cum.T
    ones_col = (lax.broadcasted_iota(jnp.int32, (L, LANES), 1) == 0).astype(BF16)
    scale = dk ** -0.5

    for h in range(M_HEADS):
        b_col = cum[:, M_HEADS + h:M_HEADS + h + 1]
        b_row = cum_t[M_HEADS + h:M_HEADS + h + 1, :]
        i_row = gval_t[h:h + 1, :]
        m_prev = m_ref[h:h + 1, 0:1]
        q = q_ref[:, h * dk:(h + 1) * dk]
        k = k_ref[:, h * dk:(h + 1) * dk]
        v_ext = jnp.concatenate([v_ref[:, h * dv:(h + 1) * dv], ones_col], axis=1)
        ct = ct_ref[h]

        dmat = jnp.where(causal, b_col - b_row + i_row, -jnp.inf)
        inter = b_col + m_prev
        m_t = jnp.maximum(inter, jnp.max(dmat, axis=1, keepdims=True))
        w_intra = jnp.exp(dmat - m_t) * scale
        w_inter = jnp.exp(inter - m_t) * scale
        s = (_dot_nt(q, k) * w_intra).astype(BF16)
        num_ext = (jnp.dot(s, v_ext, preferred_element_type=F32)
                   + w_inter * jnp.dot(q, ct.astype(BF16), preferred_element_type=F32))
        den = num_ext[:, dv:dv + 1]
        hh = num_ext[:, :dv] * (1.0 / jnp.maximum(jnp.abs(den), jnp.exp(-m_t)))
        hh = _rms(hh, hn_ref[:, h * dv:(h + 1) * dv])
        hh = hh * jax.nn.sigmoid(o_ref[:, h * dv:(h + 1) * dv].astype(F32))
        out_ref[:, h * dv:(h + 1) * dv] = hh.astype(out_ref.dtype)

        g = b_col[L - 1:L, :]
        a_row = g - b_row + i_row
        m_new = jnp.maximum(g + m_prev, jnp.max(a_row, axis=1, keepdims=True))
        keep = jnp.exp(g + m_prev - m_new)
        wk_row = jnp.exp(a_row - m_new)
        ktw = (k.astype(F32).T * wk_row).astype(BF16)
        ct_ref[h] = keep * ct + jnp.dot(ktw, v_ext, preferred_element_type=F32)
        m_ref[h:h + 1, :] = jnp.broadcast_to(m_new, (1, LANES))


def mlstm_core(proj, gates, gate_b, head_norm, bsz, seq):
    t = proj.shape[0]
    hq, hv = M_HEADS * M_DQK, M_HEADS * M_DV
    nc = seq // CHUNK
    gb = jnp.zeros((1, LANES), F32).at[0, :2 * M_HEADS].set(gate_b.reshape(-1))
    row = lambda b, c: b * nc + c
    return pl.pallas_call(
        _mlstm_kernel,
        grid=(bsz, nc),
        in_specs=[
            pl.BlockSpec((CHUNK, hq), lambda b, c: (row(b, c), 0)),
            pl.BlockSpec((CHUNK, hq), lambda b, c: (row(b, c), 1)),
            pl.BlockSpec((CHUNK, hv), lambda b, c: (row(b, c), 2 * hq // hv)),
            pl.BlockSpec((CHUNK, hv), lambda b, c: (row(b, c), 2 * hq // hv + 1)),
            pl.BlockSpec((CHUNK, LANES), lambda b, c: (row(b, c), 0)),
            pl.BlockSpec((1, LANES), lambda b, c: (0, 0)),
            pl.BlockSpec((1, hv), lambda b, c: (0, 0)),
        ],
        out_specs=pl.BlockSpec((CHUNK, hv), lambda b, c: (row(b, c), 0)),
        out_shape=jax.ShapeDtypeStruct((t, hv), BF16),
        scratch_shapes=[pltpu.VMEM((M_HEADS, M_DQK, M_DV + LANES), F32), pltpu.VMEM((8, LANES), F32)],
        compiler_params=_params("parallel", "arbitrary"),
        name="mlstm_core",
    )(proj, proj, proj, proj, gates, gb, head_norm.reshape(1, hv))


def _rotate(t, cos, sin):
    half = t.shape[1] // 2
    t1, t2 = t[:, :half], t[:, half:]
    return jnp.concatenate([t1 * cos - t2 * sin, t1 * sin + t2 * cos], axis=1)


def _retention_kernel(q_ref, k_ref, v_ref, g_ref, cos_ref, sin_ref, hn_ref, out_ref, r_ref):
    L = q_ref.shape[0]
    dk, dv = R_DQK, R_DV

    @pl.when(pl.program_id(1) == 0)
    def _():
        r_ref[...] = jnp.zeros_like(r_ref)

    cos, sin = cos_ref[...], sin_ref[...]
    rel = (lax.broadcasted_iota(jnp.int32, (L, L), 0) - lax.broadcasted_iota(jnp.int32, (L, L), 1)).astype(F32)
    t_col = lax.broadcasted_iota(jnp.int32, (L, 1), 0).astype(F32)
    s_row = lax.broadcasted_iota(jnp.int32, (1, L), 1).astype(F32)
    scale = dk ** -0.5

    for h in range(R_HEADS):
        lg = math.log(1.0 - 2.0 ** (-5.0 - h))
        dmask = jnp.where(rel >= 0, jnp.exp(lg * jnp.maximum(rel, 0.0)), 0.0) * scale
        inter_decay = jnp.exp(lg * (t_col + 1.0))
        state_w = jnp.exp(lg * (L - 1.0 - s_row)) * scale
        qr = _rotate(q_ref[:, h * dk:(h + 1) * dk].astype(F32), cos, sin).astype(BF16)
        kr = _rotate(k_ref[:, h * dk:(h + 1) * dk].astype(F32), cos, sin)
        v = v_ref[:, h * dv:(h + 1) * dv]
        r = r_ref[h]
        s = (_dot_nt(qr, kr.astype(BF16)) * dmask).astype(BF16)
        out = (jnp.dot(s, v, preferred_element_type=F32)
               + jnp.dot(qr, r.astype(BF16), preferred_element_type=F32) * inter_decay)
        ktw = (kr.T * state_w).astype(BF16)
        r_ref[h] = math.exp(lg * L) * r + jnp.dot(ktw, v, preferred_element_type=F32)

        mu = jnp.mean(out, axis=-1, keepdims=True)
        cen = out - mu
        var = jnp.mean(cen * cen, axis=-1, keepdims=True)
        o = cen * lax.rsqrt(var + EPS) * hn_ref[:, h * dv:(h + 1) * dv]
        gate = g_ref[:, h * dv:(h + 1) * dv].astype(F32)
        out_ref[:, h * dv:(h + 1) * dv] = (o * (gate * jax.nn.sigmoid(gate))).astype(out_ref.dtype)


def retention_core(proj, head_norm, bsz, seq):
    t = proj.shape[0]
    hq, hv = R_HEADS * R_DQK, R_HEADS * R_DV
    nc = seq // CHUNK
    half = R_DQK // 2
    inv = jnp.power(10000.0, -jnp.arange(half, dtype=F32) / half)
    ang = jnp.arange(seq, dtype=F32)[:, None] * inv[None, :]
    row = lambda b, c: b * nc + c
    return pl.pallas_call(
        _retention_kernel,
        grid=(bsz, nc),
        in_specs=[
            pl.BlockSpec((CHUNK, hq), lambda b, c: (row(b, c), 0)),
            pl.BlockSpec((CHUNK, hq), lambda b, c: (row(b, c), 1)),
            pl.BlockSpec((CHUNK, hv), lambda b, c: (row(b, c), 2 * hq // hv)),
            pl.BlockSpec((CHUNK, hv), lambda b, c: (row(b, c), 2 * hq // hv + 1)),
            pl.BlockSpec((CHUNK, half), lambda b, c: (c, 0)),
            pl.BlockSpec((CHUNK, half), lambda b, c: (c, 0)),
            pl.BlockSpec((1, hv), lambda b, c: (0, 0)),
        ],
        out_specs=pl.BlockSpec((CHUNK, hv), lambda b, c: (row(b, c), 0)),
        out_shape=jax.ShapeDtypeStruct((t, hv), BF16),
        scratch_shapes=[pltpu.VMEM((R_HEADS, R_DQK, R_DV), F32)],
        compiler_params=_params("parallel", "arbitrary"),
        name="retention_core",
    )(proj, proj, proj, proj, jnp.cos(ang), jnp.sin(ang), head_norm.reshape(1, hv))


def _swa_kernel(sink_ref, q_ref, kp_ref, kc_ref, vp_ref, vc_ref, out_ref):
    W, hd = WINDOW, A_HEAD_DIM
    group = A_HEADS // A_KV_HEADS
    pairs = group // 2
    has_prev = pl.program_id(1) > 0
    lane = lax.broadcasted_iota(jnp.int32, (2 * W, LANES), 1)
    qi = lax.broadcasted_iota(jnp.int32, (pairs * W, 2 * W), 0) % W
    kj = lax.broadcasted_iota(jnp.int32, (pairs * W, 2 * W), 1)
    valid = (kj > qi) & (kj <= qi + W) & (has_prev | (kj >= W))

    def pair_layout(t, kv):
        slab = t[:, (kv // 2) * LANES:(kv // 2 + 1) * LANES].astype(F32)
        other = pltpu.roll(slab, hd, axis=1)
        own_low = (kv % 2 == 0)
        low = jnp.where(lane < hd, slab if own_low else other, 0.0)
        high = jnp.where(lane >= hd, other if own_low else slab, 0.0)
        return jnp.concatenate([low, high], axis=0).astype(BF16)

    kcat = jnp.concatenate([kp_ref[...], kc_ref[...]], axis=0)
    vcat = jnp.concatenate([vp_ref[...], vc_ref[...]], axis=0)
    for kv in range(A_KV_HEADS):
        kk = pair_layout(kcat, kv)
        vv = pair_layout(vcat, kv)
        base = kv * group * hd
        qs = jnp.concatenate([q_ref[:, base + p * LANES: base + (p + 1) * LANES] for p in range(pairs)], axis=0)
        sc = _dot_nt(qs, kk) * hd ** -0.5
        probs, inv = [], []
        for half in range(2):
            s_h = jnp.where(valid, sc[:, half * 2 * W:(half + 1) * 2 * W], -jnp.inf)
            rid = lax.broadcasted_iota(jnp.int32, (pairs * W, 1), 0) // W
            sink = jnp.zeros((pairs * W, 1), F32)
            for p in range(pairs):
                sink = jnp.where(rid == p, sink_ref[kv * group + 2 * p + half], sink)
            mx = jnp.maximum(jnp.max(s_h, axis=1, keepdims=True), sink)
            pr = jnp.exp(s_h - mx)
            den = jnp.sum(pr, axis=1, keepdims=True) + jnp.exp(sink - mx)
            probs.append(pr.astype(BF16))
            inv.append(1.0 / den)
        pv = jnp.dot(jnp.concatenate(probs, axis=1), vv, preferred_element_type=F32)
        lane_o = lax.broadcasted_iota(jnp.int32, pv.shape, 1)
        pv = pv * jnp.where(lane_o < hd, inv[0], inv[1])
        for p in range(pairs):
            out_ref[:, base + p * LANES: base + (p + 1) * LANES] = pv[p * W:(p + 1) * W, :].astype(out_ref.dtype)


def swa_core(qkv, sinks, bsz, seq):
    t = qkv.shape[0]
    hq, hkv = A_HEADS * A_HEAD_DIM, A_KV_HEADS * A_HEAD_DIM
    nb = seq // WINDOW
    cur = lambda b, n, s: b * nb + n
    prev = lambda b, n, s: b * nb + jnp.maximum(n - 1, 0)
    return pl.pallas_call(
        _swa_kernel,
        grid_spec=pltpu.PrefetchScalarGridSpec(
            num_scalar_prefetch=1,
            grid=(bsz, nb),
            in_specs=[
                pl.BlockSpec((WINDOW, hq), lambda b, n, s: (cur(b, n, s), 0)),
                pl.BlockSpec((WINDOW, hkv), lambda b, n, s: (prev(b, n, s), hq // hkv)),
                pl.BlockSpec((WINDOW, hkv), lambda b, n, s: (cur(b, n, s), hq // hkv)),
                pl.BlockSpec((WINDOW, hkv), lambda b, n, s: (prev(b, n, s), hq // hkv + 1)),
                pl.BlockSpec((WINDOW, hkv), lambda b, n, s: (cur(b, n, s), hq // hkv + 1)),
            ],
            out_specs=pl.BlockSpec((WINDOW, hq), lambda b, n, s: (cur(b, n, s), 0)),
        ),
        out_shape=jax.ShapeDtypeStruct((t, hq), BF16),
        compiler_params=_params("parallel", "arbitrary"),
        name="swa_core",
    )(sinks, qkv, qkv, qkv, qkv, qkv)


def kernel(x, norm_gains, ffn_w_up, ffn_conv_w, ffn_conv_b, ffn_w_down, mlstm_w_in, mlstm_gate_b, mlstm_head_norm,
           mlstm_w_out, swa_w_qkv, swa_sinks, swa_w_out, ret_w_in, ret_head_norm, ret_w_out):
    bsz, seq, d = x.shape
    depth = norm_gains.shape[0]
    t = bsz * seq
    h = x.reshape(t, d)
    xn = rmsnorm(h, norm_gains[0, 0])
    for i in range(depth):
        kind, j = i % 3, i // 3
        if kind == 0:
            n_main = 2 * M_HEADS * (M_DQK + M_DV)
            w_gates = jnp.pad(mlstm_w_in[j, :, n_main:], ((0, 0), (0, LANES - 2 * M_HEADS)))[None]
            proj = matmul(xn, mlstm_w_in, j, n_main, BF16, 1024, 1024, "mlstm_in")
            gates = matmul(xn, w_gates, 0, LANES, F32, 1024, LANES, "mlstm_gates")
            a = mlstm_core(proj, gates, mlstm_gate_b[j], mlstm_head_norm[j], bsz, seq)
            w_out = mlstm_w_out[j]
        elif kind == 1:
            qkv = matmul(xn, swa_w_qkv, j, swa_w_qkv.shape[2], BF16, 1024, 512, "swa_in")
            a = swa_core(qkv, swa_sinks[j], bsz, seq)
            w_out = swa_w_out[j]
        else:
            proj = matmul(xn, ret_w_in, j, ret_w_in.shape[2], BF16, 1024, 1024, "ret_in")
            a = retention_core(proj, ret_head_norm[j], bsz, seq)
            w_out = ret_w_out[j]
        h, xn = out_proj(a, w_out.astype(BF16), h, norm_gains[i, 1], norm_gains[i, 2], 256, "mixer_out")
        act = ffn_up(xn, ffn_w_up[i].astype(BF16), ffn_conv_w[i], ffn_conv_b[i], seq, 512, 2816, 256)
        next_gain = norm_gains[i + 1, 0] if i + 1 < depth else None
        h, xn = out_proj(act, ffn_w_down[i].astype(BF16), h, norm_gains[i, 3], next_gain, 256, "ffn_down")
    return h.reshape(bsz, seq, d)
```

```python
import functools
import math

import jax
import jax.numpy as jnp
from jax import lax
from jax.experimental import pallas as pl
from jax.experimental.pallas import tpu as pltpu

F32 = jnp.float32
BF16 = jnp.bfloat16
EPS = 1e-6
LOG2E = math.log2(math.e)
LANES = 128
VMEM_LIMIT_BYTES = 56 * 1024 * 1024

M_HEADS, M_DQK, M_DV, M_GATE_CAP = 4, 256, 512, 15.0
A_HEADS, A_KV_HEADS, A_HEAD_DIM, WINDOW = 32, 4, 64, 128
R_HEADS, R_DQK, R_DV = 8, 256, 512
CONV_WIDTH = 3
CHUNK = 256


def _params(*semantics):
    return pltpu.CompilerParams(dimension_semantics=semantics, vmem_limit_bytes=VMEM_LIMIT_BYTES)


def _rms(y, gain):
    return y * lax.rsqrt(jnp.mean(y * y, axis=-1, keepdims=True) + EPS) * gain


def _rmsnorm_kernel(x_ref, g_ref, o_ref):
    o_ref[...] = _rms(x_ref[...], g_ref[...]).astype(o_ref.dtype)


def rmsnorm(x, gain, tm=512):
    t, d = x.shape
    return pl.pallas_call(
        _rmsnorm_kernel,
        grid=(t // tm,),
        in_specs=[pl.BlockSpec((tm, d), lambda i: (i, 0)), pl.BlockSpec((1, d), lambda i: (0, 0))],
        out_specs=pl.BlockSpec((tm, d), lambda i: (i, 0)),
        out_shape=jax.ShapeDtypeStruct((t, d), BF16),
        compiler_params=_params("parallel"),
        name="rmsnorm",
    )(x, gain.reshape(1, d))


def _rotate(t, cos, sin):
    half = t.shape[1] // 2
    t1, t2 = t[:, :half], t[:, half:]
    return jnp.concatenate([t1 * cos - t2 * sin, t1 * sin + t2 * cos], axis=1)


def _proj_kernel(*refs, kinds, sub, rot_scale):
    has_rot = any(k.startswith("rot") for k in kinds)
    x_ref, w_ref = refs[:2]
    cos_ref, sin_ref = refs[2:4] if has_rot else (None, None)
    o_ref = refs[4] if has_rot else refs[2]
    if w_ref.dtype != BF16:
        wb_ref = refs[-1]

        @pl.when(pl.program_id(1) == 0)
        def _():
            wb_ref[...] = w_ref[...].astype(BF16)
    else:
        wb_ref = w_ref
    j = pl.program_id(0)

    def run(kind):
        for c in range(o_ref.shape[1] // sub):
            sl = slice(c * sub, (c + 1) * sub)
            y = jnp.dot(x_ref[...], wb_ref[:, sl], preferred_element_type=F32)
            if kind.startswith("rot"):
                y = _rotate(y, cos_ref[...], sin_ref[...])
                if kind == "rot_scaled":
                    y = y * rot_scale
            elif kind == "silu":
                y = y * jax.nn.sigmoid(y)
            elif kind == "sigmoid":
                y = jax.nn.sigmoid(y)
            o_ref[:, sl] = y.astype(o_ref.dtype)

    for kind in sorted(set(kinds)):
        tiles = [idx for idx, k in enumerate(kinds) if k == kind]
        assert tiles == list(range(tiles[0], tiles[-1] + 1)), "tiles of one kind must be contiguous"
        pl.when((j >= tiles[0]) & (j <= tiles[-1]))(functools.partial(run, kind))


def proj(x, w_stack, layer, n, out_dtype, tm, tn, name, kinds=None, sub=256, rot=None, rot_scale=1.0):
    t, k = x.shape
    kinds = tuple(kinds) if kinds is not None else ("plain",) * (n // tn)
    assert len(kinds) == n // tn
    sub = min(sub, tn)
    in_specs = [pl.BlockSpec((tm, k), lambda j, i: (i, 0)), pl.BlockSpec((None, k, tn), lambda j, i: (layer, 0, j))]
    args = [x, w_stack]
    if rot is not None:
        cos, sin, seq = rot
        tiles_per_seq = seq // tm
        table = pl.BlockSpec((tm, sub // 2), lambda j, i: (i % tiles_per_seq, 0))
        in_specs += [table, table]
        args += [cos, sin]
    scratch = [pltpu.VMEM((k, tn), BF16)] if w_stack.dtype != BF16 else []
    return pl.pallas_call(
        functools.partial(_proj_kernel, kinds=kinds, sub=sub, rot_scale=rot_scale),
        grid=(n // tn, t // tm),
        in_specs=in_specs,
        out_specs=pl.BlockSpec((tm, tn), lambda j, i: (i, j)),
        out_shape=jax.ShapeDtypeStruct((t, n), out_dtype),
        scratch_shapes=scratch,
        compiler_params=_params("parallel", "arbitrary"),
        name=name,
    )(*args)


def _out_proj_step(a_ref, w_ref, h_ref, g_ref, gn_ref, hout_ref, xn_ref, y_prev_ref, y_cur_ref):
    y_cur_ref[...] = jnp.dot(a_ref[...], w_ref[...], preferred_element_type=F32)
    hn = h_ref[...] + _rms(y_prev_ref[...], g_ref[...])
    hout_ref[...] = hn
    if xn_ref is not None:
        xn_ref[...] = _rms(hn, gn_ref[...]).astype(BF16)


def _out_proj_kernel(a_ref, w_ref, h_ref, g_ref, gn_ref, hout_ref, *rest, emit_next):
    xn_ref = rest[0] if emit_next else None
    ya_ref, yb_ref = rest[-2:]
    i = pl.program_id(0)

    @pl.when(i == 0)
    def _():
        yb_ref[...] = jnp.zeros_like(yb_ref)

    @pl.when(i % 2 == 0)
    def _():
        _out_proj_step(a_ref, w_ref, h_ref, g_ref, gn_ref, hout_ref, xn_ref, yb_ref, ya_ref)

    @pl.when(i % 2 == 1)
    def _():
        _out_proj_step(a_ref, w_ref, h_ref, g_ref, gn_ref, hout_ref, xn_ref, ya_ref, yb_ref)


def out_proj(a, w_stack, layer, h, gain, next_gain, tm, name):
    t, k = a.shape
    d = w_stack.shape[2]
    nt = t // tm
    emit_next = next_gain is not None
    gn = next_gain if emit_next else gain
    row = pl.BlockSpec((tm, d), lambda i: (jnp.maximum(i - 1, 0), 0))
    vec = pl.BlockSpec((1, d), lambda i: (0, 0))
    out_shape = [jax.ShapeDtypeStruct((t, d), F32)]
    out_specs = [row]
    if emit_next:
        out_shape.append(jax.ShapeDtypeStruct((t, d), BF16))
        out_specs.append(row)
    res = pl.pallas_call(
        functools.partial(_out_proj_kernel, emit_next=emit_next),
        grid=(nt + 1,),
        in_specs=[pl.BlockSpec((tm, k), lambda i: (jnp.minimum(i, nt - 1), 0)),
                  pl.BlockSpec((None, k, d), lambda i: (layer, 0, 0), pipeline_mode=pl.Buffered(1)),
                  row, vec, vec],
        out_specs=out_specs,
        out_shape=out_shape,
        scratch_shapes=[pltpu.VMEM((tm, d), F32), pltpu.VMEM((tm, d), F32)],
        compiler_params=_params("arbitrary"),
        name=name,
    )(a, w_stack, h, gain.reshape(1, d), gn.reshape(1, d))
    return (res[0], res[1]) if emit_next else (res[0], None)


def _silu_mul(g, v):
    return g * jax.nn.sigmoid(g) * v


def _ffn_up_kernel(x_ref, wg_ref, wv_ref, cwg_ref, cwv_ref, cbg_ref, cbv_ref, o_ref, cg_ref, cv_ref, ug_ref, uv_ref, *,
                   tiles_per_seq, sub, rows):
    first = pl.program_id(1) % tiles_per_seq == 0
    tm, tn = o_ref.shape
    for c in range(tn // sub):
        sl = slice(c * sub, (c + 1) * sub)
        b = c % 2
        for u_ref, w_ref, carry_ref in ((ug_ref, wg_ref, cg_ref), (uv_ref, wv_ref, cv_ref)):
            u_ref[b, 0:8, :] = jnp.where(first, 0.0, carry_ref[:, sl])
            u_ref[b, 8:8 + tm, :] = jnp.dot(x_ref[...], w_ref[:, sl], preferred_element_type=F32)
            carry_ref[:, sl] = u_ref[b, tm:tm + 8, :]
        for r in range(0, tm, rows):
            def conv(u_ref, cw_ref, cb_ref):
                return (cb_ref[:, sl] + cw_ref[0:1, sl] * u_ref[b, pl.ds(6 + r, rows), :]
                        + cw_ref[1:2, sl] * u_ref[b, pl.ds(7 + r, rows), :]
                        + cw_ref[2:3, sl] * u_ref[b, pl.ds(8 + r, rows), :])
            act = _silu_mul(conv(ug_ref, cwg_ref, cbg_ref), conv(uv_ref, cwv_ref, cbv_ref))
            o_ref[r:r + rows, sl] = act.astype(o_ref.dtype)


def ffn_up(xn, w_stack, layer, conv_w, conv_b, seq, tm, tn, sub, rows=64):
    t, d = xn.shape
    f = w_stack.shape[2] // 2
    nj = f // tn
    cb = conv_b.reshape(1, 2 * f)
    return pl.pallas_call(
        functools.partial(_ffn_up_kernel, tiles_per_seq=seq // tm, sub=sub, rows=rows),
        grid=(nj, t // tm),
        in_specs=[
            pl.BlockSpec((tm, d), lambda j, i: (i, 0)),
            pl.BlockSpec((None, d, tn), lambda j, i: (layer, 0, j), pipeline_mode=pl.Buffered(1)),
            pl.BlockSpec((None, d, tn), lambda j, i: (layer, 0, j + nj), pipeline_mode=pl.Buffered(1)),
            pl.BlockSpec((CONV_WIDTH, tn), lambda j, i: (0, j)),
            pl.BlockSpec((CONV_WIDTH, tn), lambda j, i: (0, j + nj)),
            pl.BlockSpec((1, tn), lambda j, i: (0, j)),
            pl.BlockSpec((1, tn), lambda j, i: (0, j + nj)),
        ],
        out_specs=pl.BlockSpec((tm, tn), lambda j, i: (i, j)),
        out_shape=jax.ShapeDtypeStruct((t, f), BF16),
        scratch_shapes=[pltpu.VMEM((8, tn), F32), pltpu.VMEM((8, tn), F32),
                        pltpu.VMEM((2, 8 + tm, sub), F32), pltpu.VMEM((2, 8 + tm, sub), F32)],
        compiler_params=_params("parallel", "arbitrary"),
        name="ffn_up",
    )(xn, w_stack, w_stack, conv_w, conv_w, cb, cb)


def _log_sigmoid(x):
    return jnp.minimum(x, 0.0) - jnp.log1p(jnp.exp(-jnp.abs(x)))


def _dot_nt(a, b):
    return lax.dot_general(a, b, (((1,), (1,)), ((), ())), preferred_element_type=F32)


def _mlstm_kernel(q_ref, k_ref, v_ref, o_ref, gates_ref, gb_ref, hn_ref, out_ref, ct_ref, m_ref):
    L = q_ref.shape[0]
    dk, dv = M_DQK, M_DV

    @pl.when(pl.program_id(1) == 0)
    def _():
        ct_ref[...] = jnp.zeros_like(ct_ref)
        m_ref[...] = jnp.zeros_like(m_ref)

    act = M_GATE_CAP * jnp.tanh((gates_ref[...] + gb_ref[...]) / M_GATE_CAP)
    lane = lax.broadcasted_iota(jnp.int32, act.shape, 1)
    gval = jnp.where(lane >= M_HEADS, _log_sigmoid(act), act)
    rows = lax.broadcasted_iota(jnp.int32, (L, L), 0)
    cols = lax.broadcasted_iota(jnp.int32, (L, L), 1)
    causal = rows >= cols
    cum = jnp.dot(causal.astype(F32), gval, preferred_element_type=F32, precision=lax.Precision.HIGHEST)
    gval_t = gval.T
    cum_t = cum.T
    ones_col = (lax.broadcasted_iota(jnp.int32, (L, LANES), 1) == 0).astype(BF16)
    scale = dk ** -0.5

    for h in range(M_HEADS):
        b_col = cum[:, M_HEADS + h:M_HEADS + h + 1]
        b_row = cum_t[M_HEADS + h:M_HEADS + h + 1, :]
        i_row = gval_t[h:h + 1, :]
        m_prev = m_ref[h:h + 1, 0:1]
        q = q_ref[:, h * dk:(h + 1) * dk]
        k = k_ref[:, h * dk:(h + 1) * dk]
        v_ext = jnp.concatenate([v_ref[:, h * dv:(h + 1) * dv], ones_col], axis=1)
        ct = ct_ref[h]

        dmat = jnp.where(causal, b_col - b_row + i_row, -jnp.inf)
        inter = b_col + m_prev
        m_t = jnp.maximum(inter, jnp.max(dmat, axis=1, keepdims=True))
        w_intra = jnp.exp(dmat - m_t) * scale
        w_inter = jnp.exp(inter - m_t) * scale
        s = (_dot_nt(q, k) * w_intra).astype(BF16)
        num_ext = (jnp.dot(s, v_ext, preferred_element_type=F32)
                   + w_inter * jnp.dot(q, ct.astype(BF16), preferred_element_type=F32))
        den = num_ext[:, dv:dv + 1]
        hh = num_ext[:, :dv] * (1.0 / jnp.maximum(jnp.abs(den), jnp.exp(-m_t)))
        hh = _rms(hh, hn_ref[:, h * dv:(h + 1) * dv])
        hh = hh * o_ref[:, h * dv:(h + 1) * dv].astype(F32)
        out_ref[:, h * dv:(h + 1) * dv] = hh.astype(out_ref.dtype)

        g = b_col[L - 1:L, :]
        a_row = g - b_row + i_row
        m_new = jnp.maximum(g + m_prev, jnp.max(a_row, axis=1, keepdims=True))
        keep = jnp.exp(g + m_prev - m_new)
        wk_row = jnp.exp(a_row - m_new)
        ktw = (k.astype(F32).T * wk_row).astype(BF16)
        ct_ref[h] = keep * ct + jnp.dot(ktw, v_ext, preferred_element_type=F32)
        m_ref[h:h + 1, :] = jnp.broadcast_to(m_new, (1, LANES))


def mlstm_core(proj, gates, gate_b, head_norm, bsz, seq):
    t = proj.shape[0]
    hq, hv = M_HEADS * M_DQK, M_HEADS * M_DV
    nc = seq // CHUNK
    gb = jnp.zeros((1, LANES), F32).at[0, :2 * M_HEADS].set(gate_b.reshape(-1))
    row = lambda b, c: b * nc + c
    return pl.pallas_call(
        _mlstm_kernel,
        grid=(bsz, nc),
        in_specs=[
            pl.BlockSpec((CHUNK, hq), lambda b, c: (row(b, c), 0)),
            pl.BlockSpec((CHUNK, hq), lambda b, c: (row(b, c), 1)),
            pl.BlockSpec((CHUNK, hv), lambda b, c: (row(b, c), 2 * hq // hv)),
            pl.BlockSpec((CHUNK, hv), lambda b, c: (row(b, c), 2 * hq // hv + 1)),
            pl.BlockSpec((CHUNK, LANES), lambda b, c: (row(b, c), 0)),
            pl.BlockSpec((1, LANES), lambda b, c: (0, 0)),
            pl.BlockSpec((1, hv), lambda b, c: (0, 0)),
        ],
        out_specs=pl.BlockSpec((CHUNK, hv), lambda b, c: (row(b, c), 0)),
        out_shape=jax.ShapeDtypeStruct((t, hv), BF16),
        scratch_shapes=[pltpu.VMEM((M_HEADS, M_DQK, M_DV + LANES), F32), pltpu.VMEM((8, LANES), F32)],
        compiler_params=_params("parallel", "arbitrary"),
        name="mlstm_core",
    )(proj, proj, proj, proj, gates, gb, head_norm.reshape(1, hv))


def _retention_kernel(q_ref, k_ref, v_ref, g_ref, hn_ref, out_ref, r_ref, dmask_ref):
    L = q_ref.shape[0]
    dk, dv = R_DQK, R_DV
    log_decay = [math.log(1.0 - 2.0 ** (-5.0 - h)) for h in range(R_HEADS)]

    @pl.when((pl.program_id(0) == 0) & (pl.program_id(1) == 0))
    def _():
        rel = lax.broadcasted_iota(jnp.int32, (L, L), 0) - lax.broadcasted_iota(jnp.int32, (L, L), 1)
        relf = jnp.maximum(rel, 0).astype(F32)
        for h in range(R_HEADS):
            dmask_ref[h] = jnp.where(rel >= 0, jnp.exp(log_decay[h] * relf), 0.0)

    @pl.when(pl.program_id(1) == 0)
    def _():
        r_ref[...] = jnp.zeros_like(r_ref)

    t_col = lax.broadcasted_iota(jnp.int32, (L, 1), 0).astype(F32)
    s_row = lax.broadcasted_iota(jnp.int32, (1, L), 1).astype(F32)

    for h in range(R_HEADS):
        lg = log_decay[h]
        inter_decay = jnp.exp(lg * (t_col + 1.0))
        state_w = jnp.exp(lg * (L - 1.0 - s_row))
        q = q_ref[:, h * dk:(h + 1) * dk]
        k = k_ref[:, h * dk:(h + 1) * dk]
        v = v_ref[:, h * dv:(h + 1) * dv]
        r = r_ref[h]
        s = (_dot_nt(q, k) * dmask_ref[h]).astype(BF16)
        out = (jnp.dot(s, v, preferred_element_type=F32)
               + jnp.dot(q, r.astype(BF16), preferred_element_type=F32) * inter_decay)
        ktw = (k.astype(F32).T * state_w).astype(BF16)
        r_ref[h] = math.exp(lg * L) * r + jnp.dot(ktw, v, preferred_element_type=F32)

        mu = jnp.mean(out, axis=-1, keepdims=True)
        cen = out - mu
        var = jnp.mean(cen * cen, axis=-1, keepdims=True)
        o = cen * lax.rsqrt(var + EPS) * hn_ref[:, h * dv:(h + 1) * dv]
        out_ref[:, h * dv:(h + 1) * dv] = (o * g_ref[:, h * dv:(h + 1) * dv].astype(F32)).astype(out_ref.dtype)


def retention_core(proj, head_norm, bsz, seq):
    t = proj.shape[0]
    hq, hv = R_HEADS * R_DQK, R_HEADS * R_DV
    nc = seq // CHUNK
    row = lambda b, c: b * nc + c
    return pl.pallas_call(
        _retention_kernel,
        grid=(bsz, nc),
        in_specs=[
            pl.BlockSpec((CHUNK, hq), lambda b, c: (row(b, c), 0)),
            pl.BlockSpec((CHUNK, hq), lambda b, c: (row(b, c), 1)),
            pl.BlockSpec((CHUNK, hv), lambda b, c: (row(b, c), 2 * hq // hv)),
            pl.BlockSpec((CHUNK, hv), lambda b, c: (row(b, c), 2 * hq // hv + 1)),
            pl.BlockSpec((1, hv), lambda b, c: (0, 0)),
        ],
        out_specs=pl.BlockSpec((CHUNK, hv), lambda b, c: (row(b, c), 0)),
        out_shape=jax.ShapeDtypeStruct((t, hv), BF16),
        scratch_shapes=[pltpu.VMEM((R_HEADS, R_DQK, R_DV), F32), pltpu.VMEM((R_HEADS, CHUNK, CHUNK), F32)],
        compiler_params=_params("arbitrary", "arbitrary"),
        name="retention_core",
    )(proj, proj, proj, proj, head_norm.reshape(1, hv))


def rotary_tables(seq):
    half = R_DQK // 2
    inv = jnp.power(10000.0, -jnp.arange(half, dtype=F32) / half)
    ang = jnp.arange(seq, dtype=F32)[:, None] * inv[None, :]
    return jnp.cos(ang), jnp.sin(ang)


def _swa_kernel(sink_ref, q_ref, kp_ref, kc_ref, vp_ref, vc_ref, out_ref):
    W, hd = WINDOW, A_HEAD_DIM
    group = A_HEADS // A_KV_HEADS
    pairs = group // 2
    has_prev = pl.program_id(1) > 0
    lane = lax.broadcasted_iota(jnp.int32, (2 * W, LANES), 1)
    qi = lax.broadcasted_iota(jnp.int32, (pairs * W, 2 * W), 0) % W
    kj = lax.broadcasted_iota(jnp.int32, (pairs * W, 2 * W), 1)
    valid = (kj > qi) & (kj <= qi + W) & (has_prev | (kj >= W))

    def pair_layout(t, kv):
        slab = t[:, (kv // 2) * LANES:(kv // 2 + 1) * LANES].astype(F32)
        other = pltpu.roll(slab, hd, axis=1)
        own_low = (kv % 2 == 0)
        low = jnp.where(lane < hd, slab if own_low else other, 0.0)
        high = jnp.where(lane >= hd, other if own_low else slab, 0.0)
        return jnp.concatenate([low, high], axis=0).astype(BF16)

    kcat = jnp.concatenate([kp_ref[...], kc_ref[...]], axis=0)
    vcat = jnp.concatenate([vp_ref[...], vc_ref[...]], axis=0)
    for kv in range(A_KV_HEADS):
        kk = pair_layout(kcat, kv)
        vv = pair_layout(vcat, kv)
        base = kv * group * hd
        qs = jnp.concatenate([q_ref[:, base + p * LANES: base + (p + 1) * LANES] for p in range(pairs)], axis=0)
        sc = _dot_nt(qs, kk)
        probs, inv = [], []
        for half in range(2):
            s_h = jnp.where(valid, sc[:, half * 2 * W:(half + 1) * 2 * W], -jnp.inf)
            rid = lax.broadcasted_iota(jnp.int32, (pairs * W, 1), 0) // W
            sink = jnp.zeros((pairs * W, 1), F32)
            for p in range(pairs):
                sink = jnp.where(rid == p, sink_ref[kv * group + 2 * p + half], sink)
            mx = jnp.maximum(jnp.max(s_h, axis=1, keepdims=True), sink * hd ** 0.5)
            pr = jnp.exp2((s_h - mx) * (hd ** -0.5 * LOG2E))
            den = jnp.sum(pr, axis=1, keepdims=True) + jnp.exp(sink - mx * hd ** -0.5)
            probs.append(pr.astype(BF16))
            inv.append(1.0 / den)
        pv = jnp.dot(jnp.concatenate(probs, axis=1), vv, preferred_element_type=F32)
        lane_o = lax.broadcasted_iota(jnp.int32, pv.shape, 1)
        pv = pv * jnp.where(lane_o < hd, inv[0], inv[1])
        for p in range(pairs):
            out_ref[:, base + p * LANES: base + (p + 1) * LANES] = pv[p * W:(p + 1) * W, :].astype(out_ref.dtype)


def swa_core(qkv, sinks, bsz, seq):
    t = qkv.shape[0]
    hq, hkv = A_HEADS * A_HEAD_DIM, A_KV_HEADS * A_HEAD_DIM
    nb = seq // WINDOW
    cur = lambda b, n, s: b * nb + n
    prev = lambda b, n, s: b * nb + jnp.maximum(n - 1, 0)
    return pl.pallas_call(
        _swa_kernel,
        grid_spec=pltpu.PrefetchScalarGridSpec(
            num_scalar_prefetch=1,
            grid=(bsz, nb),
            in_specs=[
                pl.BlockSpec((WINDOW, hq), lambda b, n, s: (cur(b, n, s), 0)),
                pl.BlockSpec((WINDOW, hkv), lambda b, n, s: (prev(b, n, s), hq // hkv)),
                pl.BlockSpec((WINDOW, hkv), lambda b, n, s: (cur(b, n, s), hq // hkv)),
                pl.BlockSpec((WINDOW, hkv), lambda b, n, s: (prev(b, n, s), hq // hkv + 1)),
                pl.BlockSpec((WINDOW, hkv), lambda b, n, s: (cur(b, n, s), hq // hkv + 1)),
            ],
            out_specs=pl.BlockSpec((WINDOW, hq), lambda b, n, s: (cur(b, n, s), 0)),
        ),
        out_shape=jax.ShapeDtypeStruct((t, hq), BF16),
        compiler_params=_params("parallel", "arbitrary"),
        name="swa_core",
    )(sinks, qkv, qkv, qkv, qkv, qkv)


def kernel(x, norm_gains, ffn_w_up, ffn_conv_w, ffn_conv_b, ffn_w_down, mlstm_w_in, mlstm_gate_b, mlstm_head_norm,
           mlstm_w_out, swa_w_qkv, swa_sinks, swa_w_out, ret_w_in, ret_head_norm, ret_w_out):
    bsz, seq, d = x.shape
    depth = norm_gains.shape[0]
    t = bsz * seq
    h = x.reshape(t, d)
    xn = rmsnorm(h, norm_gains[0, 0])
    cos, sin = rotary_tables(seq)
    w_up_bf, w_down_bf = ffn_w_up.astype(BF16), ffn_w_down.astype(BF16)
    for i in range(depth):
        kind, j = i % 3, i // 3
        if kind == 0:
            n_main = 2 * M_HEADS * (M_DQK + M_DV)
            w_main = mlstm_w_in[:, :, :n_main].astype(BF16)
            w_gates = jnp.pad(mlstm_w_in[j, :, n_main:], ((0, 0), (0, LANES - 2 * M_HEADS)))[None]
            kinds = ["plain"] * 4 + ["sigmoid"] * 2
            projd = proj(xn, w_main, j, n_main, BF16, 1024, 1024, "mlstm_in", kinds)
            gates = proj(xn, w_gates, 0, LANES, F32, 1024, LANES, "mlstm_gates")
            a = mlstm_core(projd, gates, mlstm_gate_b[j], mlstm_head_norm[j], bsz, seq)
            w_out = mlstm_w_out
        elif kind == 1:
            qkv = proj(xn, swa_w_qkv, j, swa_w_qkv.shape[2], BF16, 1024, 512, "swa_in")
            a = swa_core(qkv, swa_sinks[j], bsz, seq)
            w_out = swa_w_out
        else:
            kinds = ["rot"] * 2 + ["rot_scaled"] * 2 + ["plain"] * 4 + ["silu"] * 4
            projd = proj(xn, ret_w_in, j, ret_w_in.shape[2], BF16, 1024, 1024, "ret_in", kinds, sub=R_DQK,
                         rot=(cos, sin, seq), rot_scale=R_DQK ** -0.5)
            a = retention_core(projd, ret_head_norm[j], bsz, seq)
            w_out = ret_w_out
        h, xn = out_proj(a, w_out.astype(BF16), j, h, norm_gains[i, 1], norm_gains[i, 2], 256, "mixer_out")
        act = ffn_up(xn, w_up_bf, i, ffn_conv_w[i], ffn_conv_b[i], seq, 512, 2816, 256)
        next_gain = norm_gains[i + 1, 0] if i + 1 < depth else None
        h, xn = out_proj(act, w_down_bf, i, h, norm_gains[i, 3], next_gain, 256, "ffn_down")
    return h.reshape(bsz, seq, d)
```

```python
import functools
import math

import jax
import jax.numpy as jnp
from jax import lax
from jax.experimental import pallas as pl
from jax.experimental.pallas import tpu as pltpu

F32 = jnp.float32
BF16 = jnp.bfloat16
EPS = 1e-6
LOG2E = math.log2(math.e)
LANES = 128
VMEM_LIMIT_BYTES = 56 * 1024 * 1024

M_HEADS, M_DQK, M_DV, M_GATE_CAP = 4, 256, 512, 15.0
A_HEADS, A_KV_HEADS, A_HEAD_DIM, WINDOW = 32, 4, 64, 128
R_HEADS, R_DQK, R_DV = 8, 256, 512
CONV_WIDTH = 3
CHUNK = 256


def _params(*semantics):
    return pltpu.CompilerParams(dimension_semantics=semantics, vmem_limit_bytes=VMEM_LIMIT_BYTES)


def _rms(y, gain):
    return y * lax.rsqrt(jnp.mean(y * y, axis=-1, keepdims=True) + EPS) * gain


def _rmsnorm_kernel(x_ref, g_ref, o_ref):
    o_ref[...] = _rms(x_ref[...], g_ref[...]).astype(o_ref.dtype)


def rmsnorm(x, gain, tm=512):
    t, d = x.shape
    return pl.pallas_call(
        _rmsnorm_kernel,
        grid=(t // tm,),
        in_specs=[pl.BlockSpec((tm, d), lambda i: (i, 0)), pl.BlockSpec((1, d), lambda i: (0, 0))],
        out_specs=pl.BlockSpec((tm, d), lambda i: (i, 0)),
        out_shape=jax.ShapeDtypeStruct((t, d), BF16),
        compiler_params=_params("parallel"),
        name="rmsnorm",
    )(x, gain.reshape(1, d))


def _rotate(t, cos, sin):
    half = t.shape[1] // 2
    t1, t2 = t[:, :half], t[:, half:]
    return jnp.concatenate([t1 * cos - t2 * sin, t1 * sin + t2 * cos], axis=1)


def _proj_kernel(*refs, kinds, sub, rot_scale):
    has_rot = any(k.startswith("rot") for k in kinds)
    x_ref, w_ref = refs[:2]
    cos_ref, sin_ref = refs[2:4] if has_rot else (None, None)
    o_ref = refs[4] if has_rot else refs[2]
    if w_ref.dtype != BF16:
        wb_ref = refs[-1]

        @pl.when(pl.program_id(1) == 0)
        def _():
            wb_ref[...] = w_ref[...].astype(BF16)
    else:
        wb_ref = w_ref
    j = pl.program_id(0)

    def run(kind):
        for c in range(o_ref.shape[1] // sub):
            sl = slice(c * sub, (c + 1) * sub)
            y = jnp.dot(x_ref[...], wb_ref[:, sl], preferred_element_type=F32)
            if kind.startswith("rot"):
                y = _rotate(y, cos_ref[...], sin_ref[...])
                if kind == "rot_scaled":
                    y = y * rot_scale
            elif kind == "silu":
                y = y * jax.nn.sigmoid(y)
            elif kind == "sigmoid":
                y = jax.nn.sigmoid(y)
            o_ref[:, sl] = y.astype(o_ref.dtype)

    for kind in sorted(set(kinds)):
        tiles = [idx for idx, k in enumerate(kinds) if k == kind]
        assert tiles == list(range(tiles[0], tiles[-1] + 1)), "tiles of one kind must be contiguous"
        pl.when((j >= tiles[0]) & (j <= tiles[-1]))(functools.partial(run, kind))


def proj(x, w_stack, layer, n, out_dtype, tm, tn, name, kinds=None, sub=256, rot=None, rot_scale=1.0):
    t, k = x.shape
    kinds = tuple(kinds) if kinds is not None else ("plain",) * (n // tn)
    assert len(kinds) == n // tn
    sub = min(sub, tn)
    in_specs = [pl.BlockSpec((tm, k), lambda j, i: (i, 0)), pl.BlockSpec((None, k, tn), lambda j, i: (layer, 0, j))]
    args = [x, w_stack]
    if rot is not None:
        cos, sin, seq = rot
        tiles_per_seq = seq // tm
        table = pl.BlockSpec((tm, sub // 2), lambda j, i: (i % tiles_per_seq, 0))
        in_specs += [table, table]
        args += [cos, sin]
    scratch = [pltpu.VMEM((k, tn), BF16)] if w_stack.dtype != BF16 else []
    return pl.pallas_call(
        functools.partial(_proj_kernel, kinds=kinds, sub=sub, rot_scale=rot_scale),
        grid=(n // tn, t // tm),
        in_specs=in_specs,
        out_specs=pl.BlockSpec((tm, tn), lambda j, i: (i, j)),
        out_shape=jax.ShapeDtypeStruct((t, n), out_dtype),
        scratch_shapes=scratch,
        compiler_params=_params("parallel", "arbitrary"),
        name=name,
    )(*args)


def _out_proj_step(a_ref, w_ref, h_ref, g_ref, gn_ref, cast_in_ref, hout_ref, xn_ref, cast_out_ref,
                   y_prev_ref, y_cur_ref):
    if y_cur_ref is not None:
        y_cur_ref[...] = jnp.dot(a_ref[...], w_ref[...], preferred_element_type=F32)
    if y_prev_ref is not None:
        hn = h_ref[...] + _rms(y_prev_ref[...], g_ref[...])
        hout_ref[...] = hn
        if xn_ref is not None:
            xn_ref[...] = _rms(hn, gn_ref[...]).astype(BF16)
    if cast_in_ref is not None:
        cast_out_ref[...] = cast_in_ref[...].astype(BF16)


def _out_proj_kernel(*refs, emit_next, with_cast):
    a_ref, w_ref, h_ref, g_ref, gn_ref = refs[:5]
    refs = list(refs[5:])
    cast_in_ref = refs.pop(0) if with_cast else None
    hout_ref = refs.pop(0)
    xn_ref = refs.pop(0) if emit_next else None
    cast_out_ref = refs.pop(0) if with_cast else None
    ya_ref, yb_ref = refs
    i = pl.program_id(0)
    last = pl.num_programs(0) - 1
    args = (a_ref, w_ref, h_ref, g_ref, gn_ref, cast_in_ref, hout_ref, xn_ref, cast_out_ref)

    @pl.when(i == 0)
    def _():
        _out_proj_step(*args, None, ya_ref)

    @pl.when((i > 0) & (i < last) & (i % 2 == 0))
    def _():
        _out_proj_step(*args, yb_ref, ya_ref)

    @pl.when((i < last) & (i % 2 == 1))
    def _():
        _out_proj_step(*args, ya_ref, yb_ref)

    @pl.when((i == last) & (i % 2 == 0))
    def _():
        _out_proj_step(*args, yb_ref, None)

    @pl.when((i == last) & (i % 2 == 1))
    def _():
        _out_proj_step(*args, ya_ref, None)


def out_proj(a, w_stack, layer, h, gain, next_gain, tm, name, cast=None):
    t, k = a.shape
    d = w_stack.shape[2]
    nt = t // tm
    emit_next = next_gain is not None
    gn = next_gain if emit_next else gain
    row = pl.BlockSpec((tm, d), lambda i: (jnp.maximum(i - 1, 0), 0))
    vec = pl.BlockSpec((1, d), lambda i: (0, 0))
    in_specs = [pl.BlockSpec((tm, k), lambda i: (jnp.minimum(i, nt - 1), 0)),
                pl.BlockSpec((None, k, d), lambda i: (layer, 0, 0), pipeline_mode=pl.Buffered(1)),
                row, vec, vec]
    args = [a, w_stack, h, gain.reshape(1, d), gn.reshape(1, d)]
    out_shape = [jax.ShapeDtypeStruct((t, d), F32)]
    out_specs = [row]
    if emit_next:
        out_shape.append(jax.ShapeDtypeStruct((t, d), BF16))
        out_specs.append(row)
    if cast is not None:
        src, src_layer = cast
        _, r, c = src.shape
        slab = r // nt
        in_specs.append(pl.BlockSpec((None, slab, c), lambda i: (src_layer, jnp.minimum(i, nt - 1), 0)))
        args.append(src)
        out_shape.append(jax.ShapeDtypeStruct((r, c), BF16))
        out_specs.append(pl.BlockSpec((slab, c), lambda i: (jnp.minimum(i, nt - 1), 0)))
    res = pl.pallas_call(
        functools.partial(_out_proj_kernel, emit_next=emit_next, with_cast=cast is not None),
        grid=(nt + 1,),
        in_specs=in_specs,
        out_specs=out_specs,
        out_shape=out_shape,
        scratch_shapes=[pltpu.VMEM((tm, d), F32), pltpu.VMEM((tm, d), F32)],
        compiler_params=_params("arbitrary"),
        name=name,
    )(*args)
    res = list(res)
    hout = res.pop(0)
    xn = res.pop(0) if emit_next else None
    converted = res.pop(0) if cast is not None else None
    return hout, xn, converted


def _silu_mul(g, v):
    return g * jax.nn.sigmoid(g) * v


def _ffn_up_kernel(x_ref, wg_ref, wv_ref, cwg_ref, cwv_ref, cbg_ref, cbv_ref, cast_in_ref, o_ref, cast_out_ref,
                   cg_ref, cv_ref, ug_ref, uv_ref, *, tiles_per_seq, sub, rows):
    first = pl.program_id(1) % tiles_per_seq == 0
    tm, tn = o_ref.shape
    cast_out_ref[...] = cast_in_ref[...].astype(BF16)
    for c in range(tn // sub):
        sl = slice(c * sub, (c + 1) * sub)
        b = c % 2
        for u_ref, w_ref, carry_ref in ((ug_ref, wg_ref, cg_ref), (uv_ref, wv_ref, cv_ref)):
            u_ref[b, 0:8, :] = jnp.where(first, 0.0, carry_ref[:, sl])
            u_ref[b, 8:8 + tm, :] = jnp.dot(x_ref[...], w_ref[:, sl], preferred_element_type=F32)
            carry_ref[:, sl] = u_ref[b, tm:tm + 8, :]
        for r in range(0, tm, rows):
            def conv(u_ref, cw_ref, cb_ref):
                return (cb_ref[:, sl] + cw_ref[0:1, sl] * u_ref[b, pl.ds(6 + r, rows), :]
                        + cw_ref[1:2, sl] * u_ref[b, pl.ds(7 + r, rows), :]
                        + cw_ref[2:3, sl] * u_ref[b, pl.ds(8 + r, rows), :])
            act = _silu_mul(conv(ug_ref, cwg_ref, cbg_ref), conv(uv_ref, cwv_ref, cbv_ref))
            o_ref[r:r + rows, sl] = act.astype(o_ref.dtype)


def ffn_up(xn, w_stack, layer, conv_w, conv_b, cast, seq, tm, tn, sub, rows=64):
    t, d = xn.shape
    f = w_stack.shape[2] // 2
    nj = f // tn
    cb = conv_b.reshape(1, 2 * f)
    src, src_layer = cast
    _, r, c = src.shape
    slab, width = r // (t // tm), c // nj
    return pl.pallas_call(
        functools.partial(_ffn_up_kernel, tiles_per_seq=seq // tm, sub=sub, rows=rows),
        grid=(nj, t // tm),
        in_specs=[
            pl.BlockSpec((tm, d), lambda j, i: (i, 0)),
            pl.BlockSpec((None, d, tn), lambda j, i: (layer, 0, j), pipeline_mode=pl.Buffered(1)),
            pl.BlockSpec((None, d, tn), lambda j, i: (layer, 0, j + nj), pipeline_mode=pl.Buffered(1)),
            pl.BlockSpec((CONV_WIDTH, tn), lambda j, i: (0, j)),
            pl.BlockSpec((CONV_WIDTH, tn), lambda j, i: (0, j + nj)),
            pl.BlockSpec((1, tn), lambda j, i: (0, j)),
            pl.BlockSpec((1, tn), lambda j, i: (0, j + nj)),
            pl.BlockSpec((None, slab, width), lambda j, i: (src_layer, i, j)),
        ],
        out_specs=[pl.BlockSpec((tm, tn), lambda j, i: (i, j)), pl.BlockSpec((slab, width), lambda j, i: (i, j))],
        out_shape=[jax.ShapeDtypeStruct((t, f), BF16), jax.ShapeDtypeStruct((r, c), BF16)],
        scratch_shapes=[pltpu.VMEM((8, tn), F32), pltpu.VMEM((8, tn), F32),
                        pltpu.VMEM((2, 8 + tm, sub), F32), pltpu.VMEM((2, 8 + tm, sub), F32)],
        compiler_params=_params("parallel", "arbitrary"),
        name="ffn_up",
    )(xn, w_stack, w_stack, conv_w, conv_w, cb, cb, src)


def _log_sigmoid(x):
    return jnp.minimum(x, 0.0) - jnp.log1p(jnp.exp(-jnp.abs(x)))


def _dot_nt(a, b):
    return lax.dot_general(a, b, (((1,), (1,)), ((), ())), preferred_element_type=F32)


def _mlstm_kernel(q_ref, k_ref, v_ref, o_ref, gates_ref, gb_ref, hn_ref, out_ref, ct_ref, m_ref):
    L = q_ref.shape[0]
    dk, dv = M_DQK, M_DV

    @pl.when(pl.program_id(1) == 0)
    def _():
        ct_ref[...] = jnp.zeros_like(ct_ref)
        m_ref[...] = jnp.zeros_like(m_ref)

    act = M_GATE_CAP * jnp.tanh((gates_ref[...] + gb_ref[...]) / M_GATE_CAP)
    lane = lax.broadcasted_iota(jnp.int32, act.shape, 1)
    gval = jnp.where(lane >= M_HEADS, _log_sigmoid(act), act)
    rows = lax.broadcasted_iota(jnp.int32, (L, L), 0)
    cols = lax.broadcasted_iota(jnp.int32, (L, L), 1)
    causal = rows >= cols
    cum = jnp.dot(causal.astype(F32), gval, preferred_element_type=F32, precision=lax.Precision.HIGHEST)
    gval_t = gval.T
    cum_t = cum.T
    ones_col = (lax.broadcasted_iota(jnp.int32, (L, LANES), 1) == 0).astype(BF16)
    scale = dk ** -0.5

    for h in range(M_HEADS):
        b_col = cum[:, M_HEADS + h:M_HEADS + h + 1]
        b_row = cum_t[M_HEADS + h:M_HEADS + h + 1, :]
        i_row = gval_t[h:h + 1, :]
        m_prev = m_ref[h:h + 1, 0:1]
        q = q_ref[:, h * dk:(h + 1) * dk]
        k = k_ref[:, h * dk:(h + 1) * dk]
        v_ext = jnp.concatenate([v_ref[:, h * dv:(h + 1) * dv], ones_col], axis=1)
        ct = ct_ref[h]

        dmat = jnp.where(causal, b_col - b_row + i_row, -jnp.inf)
        inter = b_col + m_prev
        m_t = jnp.maximum(inter, jnp.max(dmat, axis=1, keepdims=True))
        w_intra = jnp.exp(dmat - m_t) * scale
        w_inter = jnp.exp(inter - m_t) * scale
        s = (_dot_nt(q, k) * w_intra).astype(BF16)
        num_ext = (jnp.dot(s, v_ext, preferred_element_type=F32)
                   + w_inter * jnp.dot(q, ct.astype(BF16), preferred_element_type=F32))
        den = num_ext[:, dv:dv + 1]
        hh = num_ext[:, :dv] * (1.0 / jnp.maximum(jnp.abs(den), jnp.exp(-m_t)))
        hh = _rms(hh, hn_ref[:, h * dv:(h + 1) * dv])
        hh = hh * o_ref[:, h * dv:(h + 1) * dv].astype(F32)
        out_ref[:, h * dv:(h + 1) * dv] = hh.astype(out_ref.dtype)

        g = b_col[L - 1:L, :]
        a_row = g - b_row + i_row
        m_new = jnp.maximum(g + m_prev, jnp.max(a_row, axis=1, keepdims=True))
        keep = jnp.exp(g + m_prev - m_new)
        wk_row = jnp.exp(a_row - m_new)
        ktw = (k.astype(F32).T * wk_row).astype(BF16)
        ct_ref[h] = keep * ct + jnp.dot(ktw, v_ext, preferred_element_type=F32)
        m_ref[h:h + 1, :] = jnp.broadcast_to(m_new, (1, LANES))


def mlstm_core(proj, gates, gate_b, head_norm, bsz, seq):
    t = proj.shape[0]
    hq, hv = M_HEADS * M_DQK, M_HEADS * M_DV
    nc = seq // CHUNK
    gb = jnp.zeros((1, LANES), F32).at[0, :2 * M_HEADS].set(gate_b.reshape(-1))
    row = lambda b, c: b * nc + c
    return pl.pallas_call(
        _mlstm_kernel,
        grid=(bsz, nc),
        in_specs=[
            pl.BlockSpec((CHUNK, hq), lambda b, c: (row(b, c), 0)),
            pl.BlockSpec((CHUNK, hq), lambda b, c: (row(b, c), 1)),
            pl.BlockSpec((CHUNK, hv), lambda b, c: (row(b, c), 2 * hq // hv)),
            pl.BlockSpec((CHUNK, hv), lambda b, c: (row(b, c), 2 * hq // hv + 1)),
            pl.BlockSpec((CHUNK, LANES), lambda b, c: (row(b, c), 0)),
            pl.BlockSpec((1, LANES), lambda b, c: (0, 0)),
            pl.BlockSpec((1, hv), lambda b, c: (0, 0)),
        ],
        out_specs=pl.BlockSpec((CHUNK, hv), lambda b, c: (row(b, c), 0)),
        out_shape=jax.ShapeDtypeStruct((t, hv), BF16),
        scratch_shapes=[pltpu.VMEM((M_HEADS, M_DQK, M_DV + LANES), F32), pltpu.VMEM((8, LANES), F32)],
        compiler_params=_params("parallel", "arbitrary"),
        name="mlstm_core",
    )(proj, proj, proj, proj, gates, gb, head_norm.reshape(1, hv))


def _retention_kernel(q_ref, k_ref, v_ref, g_ref, hn_ref, out_ref, r_ref, dmask_ref):
    L = q_ref.shape[0]
    dk, dv = R_DQK, R_DV
    log_decay = [math.log(1.0 - 2.0 ** (-5.0 - h)) for h in range(R_HEADS)]

    @pl.when((pl.program_id(0) == 0) & (pl.program_id(1) == 0))
    def _():
        rel = lax.broadcasted_iota(jnp.int32, (L, L), 0) - lax.broadcasted_iota(jnp.int32, (L, L), 1)
        relf = jnp.maximum(rel, 0).astype(F32)
        for h in range(R_HEADS):
            dmask_ref[h] = jnp.where(rel >= 0, jnp.exp(log_decay[h] * relf), 0.0)

    @pl.when(pl.program_id(1) == 0)
    def _():
        r_ref[...] = jnp.zeros_like(r_ref)

    t_col = lax.broadcasted_iota(jnp.int32, (L, 1), 0).astype(F32)
    s_row = lax.broadcasted_iota(jnp.int32, (1, L), 1).astype(F32)

    for h in range(R_HEADS):
        lg = log_decay[h]
        inter_decay = jnp.exp(lg * (t_col + 1.0))
        state_w = jnp.exp(lg * (L - 1.0 - s_row))
        q = q_ref[:, h * dk:(h + 1) * dk]
        k = k_ref[:, h * dk:(h + 1) * dk]
        v = v_ref[:, h * dv:(h + 1) * dv]
        r = r_ref[h]
        s = (_dot_nt(q, k) * dmask_ref[h]).astype(BF16)
        out = (jnp.dot(s, v, preferred_element_type=F32)
               + jnp.dot(q, r.astype(BF16), preferred_element_type=F32) * inter_decay)
        ktw = (k.astype(F32).T * state_w).astype(BF16)
        r_ref[h] = math.exp(lg * L) * r + jnp.dot(ktw, v, preferred_element_type=F32)

        mu = jnp.mean(out, axis=-1, keepdims=True)
        cen = out - mu
        var = jnp.mean(cen * cen, axis=-1, keepdims=True)
        o = cen * lax.rsqrt(var + EPS) * hn_ref[:, h * dv:(h + 1) * dv]
        out_ref[:, h * dv:(h + 1) * dv] = (o * g_ref[:, h * dv:(h + 1) * dv].astype(F32)).astype(out_ref.dtype)


def retention_core(proj, head_norm, bsz, seq):
    t = proj.shape[0]
    hq, hv = R_HEADS * R_DQK, R_HEADS * R_DV
    nc = seq // CHUNK
    row = lambda b, c: b * nc + c
    return pl.pallas_call(
        _retention_kernel,
        grid=(bsz, nc),
        in_specs=[
            pl.BlockSpec((CHUNK, hq), lambda b, c: (row(b, c), 0)),
            pl.BlockSpec((CHUNK, hq), lambda b, c: (row(b, c), 1)),
            pl.BlockSpec((CHUNK, hv), lambda b, c: (row(b, c), 2 * hq // hv)),
            pl.BlockSpec((CHUNK, hv), lambda b, c: (row(b, c), 2 * hq // hv + 1)),
            pl.BlockSpec((1, hv), lambda b, c: (0, 0)),
        ],
        out_specs=pl.BlockSpec((CHUNK, hv), lambda b, c: (row(b, c), 0)),
        out_shape=jax.ShapeDtypeStruct((t, hv), BF16),
        scratch_shapes=[pltpu.VMEM((R_HEADS, R_DQK, R_DV), F32), pltpu.VMEM((R_HEADS, CHUNK, CHUNK), F32)],
        compiler_params=_params("arbitrary", "arbitrary"),
        name="retention_core",
    )(proj, proj, proj, proj, head_norm.reshape(1, hv))


def rotary_tables(seq):
    half = R_DQK // 2
    inv = jnp.power(10000.0, -jnp.arange(half, dtype=F32) / half)
    ang = jnp.arange(seq, dtype=F32)[:, None] * inv[None, :]
    return jnp.cos(ang), jnp.sin(ang)


def _swa_kernel(sink_ref, q_ref, kp_ref, kc_ref, vp_ref, vc_ref, out_ref):
    W, hd = WINDOW, A_HEAD_DIM
    group = A_HEADS // A_KV_HEADS
    pairs = group // 2
    has_prev = pl.program_id(1) > 0
    lane = lax.broadcasted_iota(jnp.int32, (2 * W, LANES), 1)
    qi = lax.broadcasted_iota(jnp.int32, (W, 2 * W), 0)
    kj = lax.broadcasted_iota(jnp.int32, (W, 2 * W), 1)
    valid = (kj > qi) & (kj <= qi + W) & (has_prev | (kj >= W))

    def pair_layout(t, kv):
        slab = t[:, (kv // 2) * LANES:(kv // 2 + 1) * LANES].astype(F32)
        other = pltpu.roll(slab, hd, axis=1)
        own_low = (kv % 2 == 0)
        low = jnp.where(lane < hd, slab if own_low else other, 0.0)
        high = jnp.where(lane >= hd, other if own_low else slab, 0.0)
        return jnp.concatenate([low, high], axis=0).astype(BF16)

    kcat = jnp.concatenate([kp_ref[...], kc_ref[...]], axis=0)
    vcat = jnp.concatenate([vp_ref[...], vc_ref[...]], axis=0)
    for kv in range(A_KV_HEADS):
        kk = pair_layout(kcat, kv)
        vv = pair_layout(vcat, kv)
        base = kv * group * hd
        qs = jnp.concatenate([q_ref[:, base + p * LANES: base + (p + 1) * LANES] for p in range(pairs)], axis=0)
        sc = _dot_nt(qs, kk)
        probs, inv = [], []
        for p in range(pairs):
            p_heads, p_inv = [], []
            for half in range(2):
                sink = sink_ref[kv * group + 2 * p + half]
                s_h = jnp.where(valid, sc[p * W:(p + 1) * W, half * 2 * W:(half + 1) * 2 * W], -jnp.inf)
                mx = jnp.maximum(jnp.max(s_h, axis=1, keepdims=True), sink * hd ** 0.5)
                pr = jnp.exp2((s_h - mx) * (hd ** -0.5 * LOG2E))
                den = jnp.sum(pr, axis=1, keepdims=True) + jnp.exp(sink - mx * hd ** -0.5)
                p_heads.append(pr.astype(BF16))
                p_inv.append(1.0 / den)
            probs.append(jnp.concatenate(p_heads, axis=1))
            inv.append(p_inv)
        pv = jnp.dot(jnp.concatenate(probs, axis=0), vv, preferred_element_type=F32)
        lane_o = lax.broadcasted_iota(jnp.int32, (W, LANES), 1)
        for p in range(pairs):
            o = pv[p * W:(p + 1) * W, :] * jnp.where(lane_o < hd, inv[p][0], inv[p][1])
            out_ref[:, base + p * LANES: base + (p + 1) * LANES] = o.astype(out_ref.dtype)


def swa_core(qkv, sinks, bsz, seq):
    t = qkv.shape[0]
    hq, hkv = A_HEADS * A_HEAD_DIM, A_KV_HEADS * A_HEAD_DIM
    nb = seq // WINDOW
    cur = lambda b, n, s: b * nb + n
    prev = lambda b, n, s: b * nb + jnp.maximum(n - 1, 0)
    return pl.pallas_call(
        _swa_kernel,
        grid_spec=pltpu.PrefetchScalarGridSpec(
            num_scalar_prefetch=1,
            grid=(bsz, nb),
            in_specs=[
                pl.BlockSpec((WINDOW, hq), lambda b, n, s: (cur(b, n, s), 0)),
                pl.BlockSpec((WINDOW, hkv), lambda b, n, s: (prev(b, n, s), hq // hkv)),
                pl.BlockSpec((WINDOW, hkv), lambda b, n, s: (cur(b, n, s), hq // hkv)),
                pl.BlockSpec((WINDOW, hkv), lambda b, n, s: (prev(b, n, s), hq // hkv + 1)),
                pl.BlockSpec((WINDOW, hkv), lambda b, n, s: (cur(b, n, s), hq // hkv + 1)),
            ],
            out_specs=pl.BlockSpec((WINDOW, hq), lambda b, n, s: (cur(b, n, s), 0)),
        ),
        out_shape=jax.ShapeDtypeStruct((t, hq), BF16),
        compiler_params=_params("parallel", "arbitrary"),
        name="swa_core",
    )(sinks, qkv, qkv, qkv, qkv, qkv)


def kernel(x, norm_gains, ffn_w_up, ffn_conv_w, ffn_conv_b, ffn_w_down, mlstm_w_in, mlstm_gate_b, mlstm_head_norm,
           mlstm_w_out, swa_w_qkv, swa_sinks, swa_w_out, ret_w_in, ret_head_norm, ret_w_out):
    bsz, seq, d = x.shape
    depth = norm_gains.shape[0]
    t = bsz * seq
    h = x.reshape(t, d)
    xn = rmsnorm(h, norm_gains[0, 0])
    cos, sin = rotary_tables(seq)
    for i in range(depth):
        kind, j = i % 3, i // 3
        if kind == 0:
            n_main = 2 * M_HEADS * (M_DQK + M_DV)
            w_main = mlstm_w_in[:, :, :n_main].astype(BF16)
            w_gates = jnp.pad(mlstm_w_in[j, :, n_main:], ((0, 0), (0, LANES - 2 * M_HEADS)))[None]
            kinds = ["plain"] * 4 + ["sigmoid"] * 2
            projd = proj(xn, w_main, j, n_main, BF16, 1024, 1024, "mlstm_in", kinds)
            gates = proj(xn, w_gates, 0, LANES, F32, 1024, LANES, "mlstm_gates")
            a = mlstm_core(projd, gates, mlstm_gate_b[j], mlstm_head_norm[j], bsz, seq)
            w_out = mlstm_w_out
        elif kind == 1:
            qkv = proj(xn, swa_w_qkv, j, swa_w_qkv.shape[2], BF16, 1024, 512, "swa_in")
            a = swa_core(qkv, swa_sinks[j], bsz, seq)
            w_out = swa_w_out
        else:
            kinds = ["rot"] * 2 + ["rot_scaled"] * 2 + ["plain"] * 4 + ["silu"] * 4
            projd = proj(xn, ret_w_in, j, ret_w_in.shape[2], BF16, 1024, 1024, "ret_in", kinds, sub=R_DQK,
                         rot=(cos, sin, seq), rot_scale=R_DQK ** -0.5)
            a = retention_core(projd, ret_head_norm[j], bsz, seq)
            w_out = ret_w_out
        h, xn, w_up_bf = out_proj(a, w_out.astype(BF16), j, h, norm_gains[i, 1], norm_gains[i, 2], 256, "mixer_out",
                                  cast=(ffn_w_up, i))
        act, w_down_bf = ffn_up(xn, w_up_bf[None], 0, ffn_conv_w[i], ffn_conv_b[i], (ffn_w_down, i), seq, 512, 2816, 256)
        next_gain = norm_gains[i + 1, 0] if i + 1 < depth else None
        h, xn, _ = out_proj(act, w_down_bf[None], 0, h, norm_gains[i, 3], next_gain, 256, "ffn_down")
    return h.reshape(bsz, seq, d)
```
